```python
import jax
import jax.numpy as jnp
from jax import lax
import numpy as np

D_MODEL = 2048
BATCH = 2
SEQ = 4096
DEPTH = 1
DEC_BATCH = 8
DEC_SEQ = 1
PAST_LEN = 16384
PAGE_SIZE = 128

R_HEADS = 8
R_DK = 128
R_DV = 256
R_CHUNK = 128
A_HEADS = 16
A_KV_HEADS = 4
A_HEAD_DIM = 128
A_GROUP = A_HEADS // A_KV_HEADS
I_HEADS = 16
I_DIM = 128
TOPK_MAX = 256
Q_BLOCK = 128
ROPE_THETA = 10000.0
P_HEADS = 8
P_NKEYS = 128
P_DKEY = 128
P_TOPK = 16
P_NEXPERTS = P_NKEYS * P_NKEYS
P_BLOCK = 128
EPS = 1e-6
F32 = jnp.float32

IN_SPLITS = (
    ('rq', R_HEADS * R_DK), ('rk', R_HEADS * R_DK), ('rv', R_HEADS * R_DV), ('rg', R_HEADS * R_DV),
    ('aq', A_HEADS * A_HEAD_DIM), ('ak', A_KV_HEADS * A_HEAD_DIM), ('av', A_KV_HEADS * A_HEAD_DIM),
    ('iq', I_HEADS * I_DIM), ('ik', I_DIM), ('iw', I_HEADS),
    ('gr', D_MODEL), ('ga', D_MODEL),
)
IN_WIDTH = sum(w for _, w in IN_SPLITS)

kernel_name = 'retention_dsa_peer_hybrid_step'


def _rmsnorm(x, w):
    xf = x.astype(F32)
    y = xf * lax.rsqrt(jnp.mean(xf * xf, axis=-1, keepdims=True) + EPS)
    return (y * w.astype(F32)).astype(x.dtype)


def _rope(x, pos):
    half = x.shape[-1] // 2
    freqs = ROPE_THETA ** (-jnp.arange(half, dtype=F32) / half)
    ang = pos.astype(F32)[:, None] * freqs[None, :]
    cos = jnp.cos(ang)[None, :, None, :]
    sin = jnp.sin(ang)[None, :, None, :]
    xf = x.astype(F32)
    x1, x2 = xf[..., :half], xf[..., half:]
    return jnp.concatenate([x1 * cos - x2 * sin, x2 * cos + x1 * sin], axis=-1).astype(x.dtype)


def _split_proj(z):
    parts = {}
    off = 0
    for name, width in IN_SPLITS:
        parts[name] = z[..., off:off + width]
        off += width
    return parts


def _mixer_in(x, pos, norm1_w, w_in, q_norm_w, k_norm_w):
    B, T, _ = x.shape
    xn = _rmsnorm(x, norm1_w)
    p = _split_proj(xn @ w_in)
    out = {}
    out['rq'] = _rope(p['rq'].reshape(B, T, R_HEADS, R_DK), pos)
    out['rk'] = _rope(p['rk'].reshape(B, T, R_HEADS, R_DK), pos) * (R_DK ** -0.5)
    out['rv'] = p['rv'].reshape(B, T, R_HEADS, R_DV)
    out['rg'] = p['rg']
    out['aq'] = _rope(_rmsnorm(p['aq'].reshape(B, T, A_HEADS, A_HEAD_DIM), q_norm_w), pos)
    out['ak'] = _rope(_rmsnorm(p['ak'].reshape(B, T, A_KV_HEADS, A_HEAD_DIM), k_norm_w), pos)
    out['av'] = p['av'].reshape(B, T, A_KV_HEADS, A_HEAD_DIM)
    out['iq'] = _rope(p['iq'].reshape(B, T, I_HEADS, I_DIM), pos) * (I_DIM ** -0.5)
    out['ik'] = _rope(p['ik'].reshape(B, T, 1, I_DIM), pos)[:, :, 0]
    out['iw'] = p['iw'] * (I_HEADS ** -0.5)
    out['gr'] = p['gr']
    out['ga'] = p['ga']
    return out


def _ret_log_decay():
    return jnp.log1p(-jnp.exp2(-5.0 - jnp.arange(R_HEADS, dtype=F32)))


def _ret_chunk(S, q, k, v):
    log_g = _ret_log_decay()
    C = q.shape[1]
    n = jnp.arange(C, dtype=F32)
    diff = n[:, None] - n[None, :]
    causal = diff >= 0
    dmat = jnp.where(causal, jnp.exp(jnp.where(causal, diff, 0.0)[None] * log_g[:, None, None]), 0.0)
    qf, kf, vf = q.astype(F32), k.astype(F32), v.astype(F32)
    scores = jnp.einsum('bihd,bjhd->bhij', qf, kf) * dmat[None]
    intra = jnp.einsum('bhij,bjhe->bihe', scores, vf)
    q_dec = qf * jnp.exp((n + 1.0)[:, None] * log_g[None, :])[None, :, :, None]
    cross = jnp.einsum('bihd,bhde->bihe', q_dec, S)
    k_dec = kf * jnp.exp((C - 1.0 - n)[:, None] * log_g[None, :])[None, :, :, None]
    S_new = jnp.exp(C * log_g)[None, :, None, None] * S + jnp.einsum('bjhd,bjhe->bhde', k_dec, vf)
    return S_new, intra + cross


def _retention_prompt(rq, rk, rv):
    B, T = rq.shape[:2]
    nc = T // R_CHUNK
    to_c = lambda a: jnp.moveaxis(a.reshape(B, nc, R_CHUNK, *a.shape[2:]), 1, 0)
    S0 = jnp.zeros((B, R_HEADS, R_DK, R_DV), F32)
    S, o = lax.scan(lambda S, blk: _ret_chunk(S, *blk), S0, (to_c(rq), to_c(rk), to_c(rv)))
    return jnp.moveaxis(o, 0, 1).reshape(B, T, R_HEADS, R_DV), S


def _index_select(iq, iw, ik, qpos, topk):
    s = jax.nn.relu(jnp.einsum('bqhd,bld->bqhl', iq.astype(F32), ik.astype(F32)))
    score = jnp.einsum('bqhl,bqh->bql', s, iw.astype(F32))
    kpos = jnp.arange(ik.shape[1])
    admissible = kpos[None, :] <= qpos[:, None]
    score = jnp.where(admissible[None], score, -jnp.inf)
    _, idx = lax.top_k(score, topk)
    valid = idx <= qpos[None, :, None]
    return idx, valid


def _attn_core(q, ks, vs, valid):
    B, Q = q.shape[:2]
    qg = q.reshape(B, Q, A_KV_HEADS, A_GROUP, A_HEAD_DIM).astype(F32)
    logits = jnp.einsum('bqngd,bqsnd->bqngs', qg, ks.astype(F32)) * (A_HEAD_DIM ** -0.5)
    logits = jnp.where(valid[:, :, None, None, :], logits, -1e30)
    prob = jax.nn.softmax(logits, axis=-1)
    o = jnp.einsum('bqngs,bqsnd->bqngd', prob, vs.astype(F32))
    return o.reshape(B, Q, A_HEADS * A_HEAD_DIM).astype(q.dtype)


_row_gather = jax.vmap(lambda a, i: a[i])


def _dsa_prompt(aq, ak, av, iq, iw, ik):
    B, T = aq.shape[:2]
    nb = T // Q_BLOCK
    topk = min(TOPK_MAX, T // 4)

    def blk(args):
        qb, iqb, iwb, b0 = args
        qpos = b0 * Q_BLOCK + jnp.arange(Q_BLOCK)
        idx, valid = _index_select(iqb, iwb, ik, qpos, topk)
        return _attn_core(qb, _row_gather(ak, idx), _row_gather(av, idx), valid)

    to_b = lambda a: jnp.moveaxis(a.reshape(B, nb, Q_BLOCK, *a.shape[2:]), 1, 0)
    out = lax.map(blk, (to_b(aq), to_b(iq), to_b(iw), jnp.arange(nb)))
    return jnp.moveaxis(out, 0, 1).reshape(B, T, A_HEADS * A_HEAD_DIM)


def _dsa_sample(aq, ak, av, iq, iw, ik, pool_k, pool_v, pool_kidx, page_table):
    Bd, Ts = aq.shape[:2]
    past = page_table.shape[1] * PAGE_SIZE
    L = past + Ts
    ik_past = pool_kidx[page_table].reshape(Bd, past, I_DIM)
    ik_all = jnp.concatenate([ik_past, ik.astype(ik_past.dtype)], axis=1)
    qpos = past + jnp.arange(Ts)
    idx, valid = _index_select(iq, iw, ik_all, qpos, min(TOPK_MAX, L // 4))
    is_past = idx < past
    pidx = jnp.minimum(idx, past - 1)
    phys = jnp.take_along_axis(page_table, (pidx // PAGE_SIZE).reshape(Bd, -1), axis=1).reshape(pidx.shape)
    rows = phys * PAGE_SIZE + pidx % PAGE_SIZE
    nidx = jnp.clip(idx - past, 0, Ts - 1)

    def select(pool, new):
        flat = pool.reshape(-1, *pool.shape[2:])
        return jnp.where(is_past[..., None, None], flat[rows], _row_gather(new, nidx).astype(flat.dtype))

    return _attn_core(aq, select(pool_k, ak), select(pool_v, av), valid)


def _mixer_out(x, p, ret, att, ret_gn_w, w_ret_o, w_att_o, w_out):
    B, T = x.shape[:2]
    mu = jnp.mean(ret, axis=-1, keepdims=True)
    var = jnp.mean(jnp.square(ret - mu), axis=-1, keepdims=True)
    rn = ((ret - mu) * lax.rsqrt(var + EPS)).reshape(B, T, R_HEADS * R_DV)
    rn = (rn * ret_gn_w.astype(F32) * jax.nn.silu(p['rg'].astype(F32))).astype(x.dtype)
    mixed = jax.nn.sigmoid(p['gr']) * (rn @ w_ret_o) + jax.nn.sigmoid(p['ga']) * (att @ w_att_o)
    return x + mixed @ w_out


def _peer(xn, wq, keys1, keys2, u_tab, v_tab):
    N = xn.shape[0]
    q = (xn @ wq).reshape(N, P_HEADS, 2, P_DKEY // 2).astype(F32)
    s1 = jnp.einsum('nhd,hkd->nhk', q[:, :, 0], keys1.astype(F32))
    s2 = jnp.einsum('nhd,hkd->nhk', q[:, :, 1], keys2.astype(F32))
    v1, i1 = lax.top_k(s1, P_TOPK)
    v2, i2 = lax.top_k(s2, P_TOPK)
    cand = (v1[..., :, None] + v2[..., None, :]).reshape(N, P_HEADS, P_TOPK * P_TOPK)
    sc, ci = lax.top_k(cand, P_TOPK)
    e = (jnp.take_along_axis(i1, ci // P_TOPK, axis=-1) * P_NKEYS
         + jnp.take_along_axis(i2, ci % P_TOPK, axis=-1))
    g = jax.nn.softmax(sc, axis=-1)
    act = jax.nn.gelu(jnp.einsum('nhkd,nd->nhk', u_tab[e].astype(F32), xn.astype(F32)), approximate=False)
    out = jnp.einsum('nhk,nhkd->nd', g * act, v_tab[e].astype(F32))
    return out.astype(xn.dtype)


def _prompt_layer(x, W):
    (norm1_w, w_in, q_norm_w, k_norm_w, ret_gn_w, w_ret_o, w_att_o, w_out,
     norm2_w, peer_wq, peer_keys1, peer_keys2, peer_u, peer_v) = W
    B, T, D = x.shape
    pos = jnp.arange(T)
    p = _mixer_in(x, pos, norm1_w, w_in, q_norm_w, k_norm_w)
    ret, S = _retention_prompt(p['rq'], p['rk'], p['rv'])
    att = _dsa_prompt(p['aq'], p['ak'], p['av'], p['iq'], p['iw'], p['ik'])
    h = _mixer_out(x, p, ret, att, ret_gn_w, w_ret_o, w_att_o, w_out)
    hn = _rmsnorm(h, norm2_w).reshape(-1, P_BLOCK, D)
    f = lax.map(lambda t: _peer(t, peer_wq, peer_keys1, peer_keys2, peer_u, peer_v), hn).reshape(B, T, D)
    return h + f, p['ak'], p['av'], p['ik'], S


def _sample_layer(x, pool_k, pool_v, pool_kidx, S_in, page_table, W):
    (norm1_w, w_in, q_norm_w, k_norm_w, ret_gn_w, w_ret_o, w_att_o, w_out,
     norm2_w, peer_wq, peer_keys1, peer_keys2, peer_u, peer_v) = W
    Bd, Ts, D = x.shape
    pos = page_table.shape[1] * PAGE_SIZE + jnp.arange(Ts)
    p = _mixer_in(x, pos, norm1_w, w_in, q_norm_w, k_norm_w)
    S, ret = _ret_chunk(S_in.astype(F32), p['rq'], p['rk'], p['rv'])
    att = _dsa_sample(p['aq'], p['ak'], p['av'], p['iq'], p['iw'], p['ik'], pool_k, pool_v, pool_kidx, page_table)
    h = _mixer_out(x, p, ret, att, ret_gn_w, w_ret_o, w_att_o, w_out)
    hn = _rmsnorm(h, norm2_w).reshape(-1, D)
    f = _peer(hn, peer_wq, peer_keys1, peer_keys2, peer_u, peer_v).reshape(Bd, Ts, D)
    return h + f, p['ak'], p['av'], p['ik'], S


def setup_inputs(seed: int = 0) -> dict:
    key = jax.random.key(seed)
    ks = jax.random.split(key, 24)
    nrm = lambda k, shape, scale: jax.random.normal(k, shape, F32) * scale
    gain = lambda k, shape: 1.0 + 0.01 * jax.random.normal(k, shape, F32)
    n_pages = PAST_LEN // PAGE_SIZE
    n_used = DEC_BATCH * n_pages
    n_pool = (n_used * 5) // 4
    page_table = jax.random.permutation(ks[6], n_pool)[:n_used].reshape(DEC_BATCH, n_pages).astype(jnp.int32)
    return {
        'x_prompt': nrm(ks[0], (BATCH, SEQ, D_MODEL), 1.0),
        'x_sample': nrm(ks[1], (DEC_BATCH, DEC_SEQ, D_MODEL), 1.0),
        'cache_k': nrm(ks[2], (DEPTH, n_pool, PAGE_SIZE, A_KV_HEADS, A_HEAD_DIM), 1.0),
        'cache_v': nrm(ks[3], (DEPTH, n_pool, PAGE_SIZE, A_KV_HEADS, A_HEAD_DIM), 1.0),
        'cache_k_idx': nrm(ks[4], (DEPTH, n_pool, PAGE_SIZE, I_DIM), 1.0),
        'state_ret': nrm(ks[5], (DEPTH, DEC_BATCH, R_HEADS, R_DK, R_DV), 0.1),
        'page_table': page_table,
        'norm1_w': gain(ks[7], (DEPTH, D_MODEL)),
        'w_in': nrm(ks[8], (DEPTH, D_MODEL, IN_WIDTH), D_MODEL ** -0.5),
        'q_norm_w': gain(ks[9], (DEPTH, A_HEAD_DIM)),
        'k_norm_w': gain(ks[10], (DEPTH, A_HEAD_DIM)),
        'ret_gn_w': gain(ks[11], (DEPTH, R_HEADS * R_DV)),
        'w_ret_o': nrm(ks[12], (DEPTH, R_HEADS * R_DV, D_MODEL), (R_HEADS * R_DV) ** -0.5),
        'w_att_o': nrm(ks[13], (DEPTH, A_HEADS * A_HEAD_DIM, D_MODEL), (A_HEADS * A_HEAD_DIM) ** -0.5),
        'w_out': nrm(ks[14], (DEPTH, D_MODEL, D_MODEL), D_MODEL ** -0.5),
        'norm2_w': gain(ks[15], (DEPTH, D_MODEL)),
        'peer_wq': nrm(ks[16], (DEPTH, D_MODEL, P_HEADS * P_DKEY), D_MODEL ** -0.5),
        'peer_keys1': nrm(ks[17], (DEPTH, P_HEADS, P_NKEYS, P_DKEY // 2), (P_DKEY // 2) ** -0.5),
        'peer_keys2': nrm(ks[18], (DEPTH, P_HEADS, P_NKEYS, P_DKEY // 2), (P_DKEY // 2) ** -0.5),
        'peer_u': nrm(ks[19], (DEPTH, P_NEXPERTS, D_MODEL), D_MODEL ** -0.5),
        'peer_v': nrm(ks[20], (DEPTH, P_NEXPERTS, D_MODEL), (P_HEADS * P_TOPK) ** -0.5),
    }


def reference(x_prompt, x_sample, cache_k, cache_v, cache_k_idx, state_ret, page_table,
              norm1_w, w_in, q_norm_w, k_norm_w, ret_gn_w, w_ret_o, w_att_o, w_out,
              norm2_w, peer_wq, peer_keys1, peer_keys2, peer_u, peer_v):
    yp, ys = x_prompt, x_sample
    kp_l, vp_l, ikp_l, sp_l, ks_l, vs_l, iks_l, ss_l = [], [], [], [], [], [], [], []
    for layer in range(DEPTH):
        W = (norm1_w[layer], w_in[layer], q_norm_w[layer], k_norm_w[layer], ret_gn_w[layer],
             w_ret_o[layer], w_att_o[layer], w_out[layer], norm2_w[layer], peer_wq[layer],
             peer_keys1[layer], peer_keys2[layer], peer_u[layer], peer_v[layer])
        yp, kp, vp, ikp, sp = _prompt_layer(yp, W)
        ys, kn, vn, ikn, sn = _sample_layer(ys, cache_k[layer], cache_v[layer], cache_k_idx[layer],
                                            state_ret[layer], page_table, W)
        kp_l.append(kp); vp_l.append(vp); ikp_l.append(ikp); sp_l.append(sp)
        ks_l.append(kn); vs_l.append(vn); iks_l.append(ikn); ss_l.append(sn)
    return (yp, ys,
            jnp.stack(kp_l), jnp.stack(vp_l), jnp.stack(ikp_l), jnp.stack(sp_l),
            jnp.stack(ks_l), jnp.stack(vs_l), jnp.stack(iks_l), jnp.stack(ss_l))
```

```python
import functools

import jax
import jax.numpy as jnp
from jax import lax
from jax.experimental import pallas as pl
from jax.experimental.pallas import tpu as pltpu

F32 = jnp.float32
BF16 = jnp.bfloat16
I32 = jnp.int32

D_MODEL = 2048
PAGE = 128
R_HEADS, R_DK, R_DV, R_CHUNK = 8, 128, 256, 128
A_HEADS, A_KV, A_DH = 16, 4, 128
A_GROUP = A_HEADS // A_KV
I_HEADS, I_DIM = 16, 128
TOPK_MAX = 256
Q_BLOCK = 128
ROPE_THETA = 10000.0
P_HEADS, P_NKEYS, P_DKEY, P_TOPK = 8, 128, 128, 16
EPS = 1e-6

LANE = 128
VMEM_LIMIT = 56 * 1024 * 1024
NEG_BIG = -1e30
INT_MIN = -2147483648
NEGINF_KEY = INT_MIN + 0x7FFFFF

SEG = dict(rq=(0, 8), rk=(8, 8), rvg=(16, 32), aq=(48, 16), ak=(64, 4), av=(68, 4),
           iq=(72, 16), ik=(88, 1), iw=(89, 1), gates=(92, 32))
IW_COL = 11392
GATE_COL = 11408


def _cparams(sem):
    return pltpu.CompilerParams(dimension_semantics=sem, vmem_limit_bytes=VMEM_LIMIT)


def _rmsnorm_kernel(x_ref, w_ref, o_ref):
    x = x_ref[...]
    ms = jnp.mean(x * x, axis=-1, keepdims=True)
    o_ref[...] = (x * lax.rsqrt(ms + EPS) * w_ref[...]).astype(o_ref.dtype)


def _rmsnorm(x, w, tm):
    n, d = x.shape
    return pl.pallas_call(
        _rmsnorm_kernel,
        grid=(n // tm,),
        in_specs=[pl.BlockSpec((tm, d), lambda i: (i, 0)), pl.BlockSpec((1, d), lambda i: (0, 0))],
        out_specs=pl.BlockSpec((tm, d), lambda i: (i, 0)),
        out_shape=jax.ShapeDtypeStruct((n, d), BF16),
        compiler_params=_cparams(("parallel",)),
        name="rmsnorm",
    )(x, w.reshape(1, d))


def _proj_kernel(x_ref, w_ref, cos_ref, sin_ref, nw_ref, *out_refs, mode, scale, heads):
    z = jnp.dot(x_ref[...], w_ref[...], preferred_element_type=F32)
    if mode in ("rope", "norm_rope"):
        cos, sin = cos_ref[...], sin_ref[...]
        parts = []
        for h in range(heads):
            zh = z[:, h * LANE:(h + 1) * LANE]
            if mode == "norm_rope":
                zh = zh * lax.rsqrt(jnp.mean(zh * zh, axis=-1, keepdims=True) + EPS) * nw_ref[...]
            zh = zh * cos + pltpu.roll(zh, LANE // 2, 1) * sin
            if scale != 1.0:
                zh = zh * scale
            parts.append(zh)
        z = jnp.concatenate(parts, axis=1) if heads > 1 else parts[0]
    elif mode == "sigmoid":
        z = jax.nn.sigmoid(z)
    elif scale != 1.0:
        z = z * scale
    for o in out_refs:
        o[...] = z.astype(o.dtype)


def _proj(xn, w, seg, cos2, sin2, nw, *, mode, scale, out_dtypes, tm, tn):
    n, d = xn.shape
    c0, nblk = seg
    width = nblk * LANE
    tn = min(tn, width)
    c0t = (c0 * LANE) // tn
    assert (c0 * LANE) % tn == 0 and width % tn == 0 and n % tm == 0
    pos_blocks = cos2.shape[0] // tm
    kern = functools.partial(_proj_kernel, mode=mode, scale=scale, heads=tn // LANE)
    outs = pl.pallas_call(
        kern,
        grid=(width // tn, n // tm),
        in_specs=[
            pl.BlockSpec((tm, d), lambda j, i: (i, 0)),
            pl.BlockSpec((d, tn), lambda j, i: (0, c0t + j)),
            pl.BlockSpec((tm, LANE), lambda j, i: (i % pos_blocks, 0)),
            pl.BlockSpec((tm, LANE), lambda j, i: (i % pos_blocks, 0)),
            pl.BlockSpec((1, LANE), lambda j, i: (0, 0)),
        ],
        out_specs=[pl.BlockSpec((tm, tn), lambda j, i: (i, j)) for _ in out_dtypes],
        out_shape=[jax.ShapeDtypeStruct((n, width), dt) for dt in out_dtypes],
        compiler_params=_cparams(("parallel", "parallel")),
        name="proj_" + mode,
    )(xn, w, cos2, sin2, nw)
    return outs


def _rope_tables(pos):
    half = LANE // 2
    freqs = ROPE_THETA ** (-jnp.arange(half, dtype=F32) / half)
    ang = pos.astype(F32)[:, None] * freqs[None, :]
    cos, sin = jnp.cos(ang), jnp.sin(ang)
    return jnp.concatenate([cos, cos], axis=1), jnp.concatenate([-sin, sin], axis=1)


def _mixer_in(x2d, pos_rows, norm1_w, w_bf, q_norm_w, k_norm_w, tm):
    xn = _rmsnorm(x2d, norm1_w, tm)
    cos2, sin2 = _rope_tables(pos_rows)
    ones = jnp.ones((1, LANE), F32)
    p = functools.partial(_proj, xn, w_bf, cos2=cos2, sin2=sin2, tm=tm, tn=512)
    out = {}
    out["rq"], = p(seg=SEG["rq"], nw=ones, mode="rope", scale=1.0, out_dtypes=(BF16,))
    out["rk"], = p(seg=SEG["rk"], nw=ones, mode="rope", scale=R_DK ** -0.5, out_dtypes=(BF16,))
    out["rvg"], = p(seg=SEG["rvg"], nw=ones, mode="plain", scale=1.0, out_dtypes=(BF16,))
    out["aq"], = p(seg=SEG["aq"], nw=q_norm_w.reshape(1, LANE), mode="norm_rope", scale=1.0, out_dtypes=(BF16,))
    out["ak"], out["ak_bf"] = p(seg=SEG["ak"], nw=k_norm_w.reshape(1, LANE), mode="norm_rope", scale=1.0,
                                 out_dtypes=(F32, BF16))
    out["av"], out["av_bf"] = p(seg=SEG["av"], nw=ones, mode="plain", scale=1.0, out_dtypes=(F32, BF16))
    out["iq"], = p(seg=SEG["iq"], nw=ones, mode="rope", scale=I_DIM ** -0.5, out_dtypes=(BF16,))
    out["ik"], out["ik_bf"] = p(seg=SEG["ik"], nw=ones, mode="rope", scale=1.0, out_dtypes=(F32, BF16))
    out["iw"], = p(seg=SEG["iw"], nw=ones, mode="plain", scale=I_HEADS ** -0.5, out_dtypes=(F32,))
    out["gates"], = p(seg=SEG["gates"], nw=ones, mode="sigmoid", scale=1.0, out_dtypes=(BF16,))
    return out


def _ret_tables(chunk):
    log_g = jnp.log1p(-jnp.exp2(-5.0 - jnp.arange(R_HEADS, dtype=F32)))
    n = jnp.arange(chunk, dtype=F32)
    diff = n[:, None] - n[None, :]
    causal = diff >= 0
    dmat = jnp.where(causal, jnp.exp(jnp.where(causal, diff, 0.0)[None] * log_g[:, None, None]), 0.0)
    qdec = jnp.exp((n + 1.0)[None, :] * log_g[:, None])
    kdec = jnp.exp((chunk - 1.0 - n)[None, :] * log_g[:, None])
    gc = jnp.exp(chunk * log_g)
    return dmat, qdec, kdec, gc


def _ret_kernel(q_ref, k_ref, v_ref, g_ref, gnw_ref, dmat_ref, qdec_ref, kdec_ref, gc_ref, s0_ref,
                o_ref, s_ref):
    c = pl.program_id(2)

    @pl.when(c == 0)
    def _():
        s_ref[...] = s0_ref[...]

    S = s_ref[0, 0]
    q, k, v = q_ref[...], k_ref[...], v_ref[...]
    nt = (((1,), (1,)), ((), ()))
    tn = (((0,), (0,)), ((), ()))
    scores = lax.dot_general(q, k, nt, preferred_element_type=F32) * dmat_ref[0]
    intra = jnp.dot(scores.astype(BF16), v, preferred_element_type=F32)
    qd = (q.astype(F32) * qdec_ref[0]).astype(BF16)
    cross = jnp.dot(qd, S.astype(BF16), preferred_element_type=F32)
    kd = (k.astype(F32) * kdec_ref[0]).astype(BF16)
    s_ref[0, 0] = gc_ref[0] * S + lax.dot_general(kd, v, tn, preferred_element_type=F32)
    o = intra + cross
    mu = jnp.mean(o, axis=-1, keepdims=True)
    var = jnp.mean(jnp.square(o - mu), axis=-1, keepdims=True)
    g = g_ref[...].astype(F32)
    rn = (o - mu) * lax.rsqrt(var + EPS) * gnw_ref[...] * (g * jax.nn.sigmoid(g))
    o_ref[...] = rn.astype(o_ref.dtype)


def _retention_prompt(rq, rk, rvg, gn_w, s0, bsz, seq):
    C = R_CHUNK
    nc = seq // C
    dmat, qdec, kdec, gc = _ret_tables(C)
    qdec = jnp.broadcast_to(qdec[:, :, None], (R_HEADS, C, R_DK))
    kdec = jnp.broadcast_to(kdec[:, :, None], (R_HEADS, C, R_DK))
    gc = jnp.broadcast_to(gc[:, None, None], (R_HEADS, 1, R_DV))
    row = lambda b, h, c: b * nc + c
    rn, S = pl.pallas_call(
        _ret_kernel,
        grid=(bsz, R_HEADS, nc),
        in_specs=[
            pl.BlockSpec((C, R_DK), lambda b, h, c: (row(b, h, c), h)),
            pl.BlockSpec((C, R_DK), lambda b, h, c: (row(b, h, c), h)),
            pl.BlockSpec((C, R_DV), lambda b, h, c: (row(b, h, c), h)),
            pl.BlockSpec((C, R_DV), lambda b, h, c: (row(b, h, c), R_HEADS + h)),
            pl.BlockSpec((1, R_DV), lambda b, h, c: (0, h)),
            pl.BlockSpec((1, C, C), lambda b, h, c: (h, 0, 0)),
            pl.BlockSpec((1, C, R_DK), lambda b, h, c: (h, 0, 0)),
            pl.BlockSpec((1, C, R_DK), lambda b, h, c: (h, 0, 0)),
            pl.BlockSpec((1, 1, R_DV), lambda b, h, c: (h, 0, 0)),
            pl.BlockSpec((1, 1, R_DK, R_DV), lambda b, h, c: (b, h, 0, 0)),
        ],
        out_specs=[
            pl.BlockSpec((C, R_DV), lambda b, h, c: (row(b, h, c), h)),
            pl.BlockSpec((1, 1, R_DK, R_DV), lambda b, h, c: (b, h, 0, 0)),
        ],
        out_shape=[jax.ShapeDtypeStruct((bsz * seq, R_HEADS * R_DV), BF16),
                   jax.ShapeDtypeStruct((bsz, R_HEADS, R_DK, R_DV), F32)],
        compiler_params=_cparams(("parallel", "parallel", "arbitrary")),
        name="retention",
    )(rq, rk, rvg, rvg, gn_w.reshape(1, -1), dmat, qdec, kdec, gc, s0)
    return rn, S


def _ret_step_kernel(qc_ref, kc_ref, v_ref, g_ref, gnw_ref, gam_ref, s0_ref, o_ref, s_ref):
    for h in range(R_HEADS):
        S = s0_ref[0, h]
        qc, kc = qc_ref[0, h], kc_ref[0, h]
        v = v_ref[0, h]
        gam = gam_ref[h]
        qk = jnp.sum(qc * kc, axis=0, keepdims=True)
        cross = jnp.sum((qc * gam) * S, axis=0, keepdims=True)
        o = qk * v + cross
        s_ref[0, h] = gam * S + kc * v
        mu = jnp.mean(o, axis=-1, keepdims=True)
        var = jnp.mean(jnp.square(o - mu), axis=-1, keepdims=True)
        g = g_ref[0, h]
        o_ref[0, h] = (o - mu) * lax.rsqrt(var + EPS) * gnw_ref[0, h] * (g * jax.nn.sigmoid(g))


def _retention_sample(rq, rk, rvg, gn_w, s0):
    bd = rq.shape[0]
    _, qdec, _, _ = _ret_tables(1)
    gam = jnp.broadcast_to(qdec[:, :, None], (R_HEADS, 1, R_DV)).astype(F32)
    qc = rq.astype(F32).reshape(bd, R_HEADS, R_DK, 1)
    kc = rk.astype(F32).reshape(bd, R_HEADS, R_DK, 1)
    rv = rvg[:, :R_HEADS * R_DV].astype(F32).reshape(bd, R_HEADS, 1, R_DV)
    rg = rvg[:, R_HEADS * R_DV:].astype(F32).reshape(bd, R_HEADS, 1, R_DV)
    gnw = gn_w.reshape(1, R_HEADS, 1, R_DV)
    col = pl.BlockSpec((1, R_HEADS, R_DK, 1), lambda b: (b, 0, 0, 0))
    rowv = pl.BlockSpec((1, R_HEADS, 1, R_DV), lambda b: (b, 0, 0, 0))
    st = pl.BlockSpec((1, R_HEADS, R_DK, R_DV), lambda b: (b, 0, 0, 0))
    rn, S = pl.pallas_call(
        _ret_step_kernel,
        grid=(bd,),
        in_specs=[col, col, rowv, rowv,
                  pl.BlockSpec((1, R_HEADS, 1, R_DV), lambda b: (0, 0, 0, 0)),
                  pl.BlockSpec((R_HEADS, 1, R_DV), lambda b: (0, 0, 0)), st],
        out_specs=[rowv, st],
        out_shape=[jax.ShapeDtypeStruct((bd, R_HEADS, 1, R_DV), F32),
                   jax.ShapeDtypeStruct((bd, R_HEADS, R_DK, R_DV), F32)],
        compiler_params=_cparams(("parallel",)),
        name="retention_step",
    )(qc, kc, rv, rg, gnw, gam, s0)
    return rn.reshape(bd, R_HEADS * R_DV).astype(BF16), S


def _order_key(x):
    bits = pltpu.bitcast(x, I32)
    return bits ^ ((bits >> 31) & 0x7FFFFFFF)


def _kth_largest_key(count_ge, k, shape):
    cur = jnp.where(count_ge(jnp.zeros(shape, I32)) >= k, 0, INT_MIN).astype(I32)

    def body(i, cur):
        cand = cur | jnp.left_shift(jnp.int32(1), 30 - i)
        return jnp.where(count_ge(cand) >= k, cand, cur)

    return lax.fori_loop(0, 31, body, cur)


KEY_UNIT = 512


def _dsa_prompt_kernel(iq_ref, iw_ref, ik_ref, aq_ref, ak_ref, av_ref, o_ref,
                       keys_ref, bias_ref, m_ref, l_ref, acc_ref, *, topk):
    qb = pl.program_id(1)
    nunits = (qb * Q_BLOCK + Q_BLOCK + KEY_UNIT - 1) // KEY_UNIT
    nt = (((1,), (1,)), ((), ()))

    iq = iq_ref[...]
    rhs = jnp.concatenate([iq[:, h * LANE:(h + 1) * LANE] for h in range(I_HEADS)], axis=0)
    w_t = iw_ref[...].T
    t_idx = qb * Q_BLOCK + lax.broadcasted_iota(I32, (Q_BLOCK, Q_BLOCK), 1)

    def score_chunk(kb, carry):
        off = pl.multiple_of(kb * Q_BLOCK, Q_BLOCK)
        ikc = ik_ref[0, pl.ds(off, Q_BLOCK), :]
        prod = lax.dot_general(ikc, rhs, nt, preferred_element_type=F32)
        sc = jnp.zeros((Q_BLOCK, Q_BLOCK), F32)
        for h in range(I_HEADS):
            sc = sc + jnp.maximum(prod[:, h * Q_BLOCK:(h + 1) * Q_BLOCK], 0.0) * w_t[h:h + 1, :]
        s_idx = off + lax.broadcasted_iota(I32, (Q_BLOCK, Q_BLOCK), 0)
        sc = jnp.where(s_idx <= t_idx, sc, -jnp.inf)
        keys_ref[pl.ds(off, Q_BLOCK), :] = _order_key(sc)
        return carry

    lax.fori_loop(0, nunits * (KEY_UNIT // Q_BLOCK), score_chunk, 0)

    def count_ge(cand):
        def body(u, acc):
            off = pl.multiple_of(u * KEY_UNIT, KEY_UNIT)
            blk = keys_ref[pl.ds(off, KEY_UNIT), :]
            hit = jnp.where(blk >= cand, 1, 0).astype(I32)
            return acc + jnp.sum(hit.reshape(KEY_UNIT // 8, 8, Q_BLOCK), axis=0)
        acc = lax.fori_loop(0, nunits, body, jnp.zeros((8, Q_BLOCK), I32))
        return jnp.sum(acc, axis=0, keepdims=True)

    thr = _kth_largest_key(count_ge, topk, (1, Q_BLOCK))
    thr = jnp.maximum(thr, NEGINF_KEY + 1)

    def bias_chunk(kb, carry):
        off = pl.multiple_of(kb * Q_BLOCK, Q_BLOCK)
        sel = keys_ref[pl.ds(off, Q_BLOCK), :] >= thr
        bias_ref[:, pl.ds(off, Q_BLOCK)] = jnp.where(sel, 0.0, NEG_BIG).astype(F32).T
        return carry

    lax.fori_loop(0, nunits * (KEY_UNIT // Q_BLOCK), bias_chunk, 0)

    aq = aq_ref[...]
    scale = A_DH ** -0.5
    rows = A_GROUP * Q_BLOCK
    for n in range(A_KV):
        qg = jnp.concatenate([aq[:, (n * A_GROUP + g) * A_DH:(n * A_GROUP + g + 1) * A_DH]
                              for g in range(A_GROUP)], axis=0)
        m_ref[...] = jnp.full((rows, 1), NEG_BIG, F32)
        l_ref[...] = jnp.zeros((rows, 1), F32)
        acc_ref[...] = jnp.zeros((rows, A_DH), F32)

        def flash(u, carry, n=n, qg=qg):
            off = pl.multiple_of(u * KEY_UNIT, KEY_UNIT)
            kch = ak_ref[0, pl.ds(off, KEY_UNIT), n * A_DH:(n + 1) * A_DH]
            vch = av_ref[0, pl.ds(off, KEY_UNIT), n * A_DH:(n + 1) * A_DH]
            s = lax.dot_general(qg, kch, nt, preferred_element_type=F32) * scale
            b = bias_ref[:, pl.ds(off, KEY_UNIT)]
            s = s + jnp.concatenate([b] * A_GROUP, axis=0)
            m_old = m_ref[...]
            m_new = jnp.maximum(m_old, jnp.max(s, axis=-1, keepdims=True))
            p = jnp.exp(s - m_new)
            alpha = jnp.exp(m_old - m_new)
            l_ref[...] = alpha * l_ref[...] + jnp.sum(p, axis=-1, keepdims=True)
            acc_ref[...] = alpha * acc_ref[...] + jnp.dot(p.astype(BF16), vch, preferred_element_type=F32)
            m_ref[...] = m_new
            return carry

        lax.fori_loop(0, nunits, flash, 0)
        o = acc_ref[...] / l_ref[...]
        for g in range(A_GROUP):
            hh = n * A_GROUP + g
            o_ref[:, hh * A_DH:(hh + 1) * A_DH] = o[g * Q_BLOCK:(g + 1) * Q_BLOCK].astype(o_ref.dtype)


def _dsa_prompt(iq, iw, ik_bf, aq, ak_bf, av_bf, bsz, seq):
    nb = seq // Q_BLOCK
    topk = min(TOPK_MAX, seq // 4)
    assert seq % KEY_UNIT == 0
    ik3 = ik_bf.reshape(bsz, seq, I_DIM)
    ak3 = ak_bf.reshape(bsz, seq, A_KV * A_DH)
    av3 = av_bf.reshape(bsz, seq, A_KV * A_DH)
    rowblk = lambda b, q: (b * nb + q, 0)
    rows = A_GROUP * Q_BLOCK
    return pl.pallas_call(
        functools.partial(_dsa_prompt_kernel, topk=topk),
        grid=(bsz, nb),
        in_specs=[
            pl.BlockSpec((Q_BLOCK, I_HEADS * I_DIM), rowblk),
            pl.BlockSpec((Q_BLOCK, LANE), rowblk),
            pl.BlockSpec((1, seq, I_DIM), lambda b, q: (b, 0, 0)),
            pl.BlockSpec((Q_BLOCK, A_HEADS * A_DH), rowblk),
            pl.BlockSpec((1, seq, A_KV * A_DH), lambda b, q: (b, 0, 0)),
            pl.BlockSpec((1, seq, A_KV * A_DH), lambda b, q: (b, 0, 0)),
        ],
        out_specs=pl.BlockSpec((Q_BLOCK, A_HEADS * A_DH), rowblk),
        out_shape=jax.ShapeDtypeStruct((bsz * seq, A_HEADS * A_DH), BF16),
        scratch_shapes=[
            pltpu.VMEM((seq, Q_BLOCK), I32),
            pltpu.VMEM((Q_BLOCK, seq), F32),
            pltpu.VMEM((rows, 1), F32),
            pltpu.VMEM((rows, 1), F32),
            pltpu.VMEM((rows, A_DH), F32),
        ],
        compiler_params=_cparams(("parallel", "arbitrary")),
        name="dsa_prompt",
    )(iq, iw, ik3, aq, ak3, av3)


IDX_PAGES = 8
ATT_PAGES = 4


def _dsa_sample_index_kernel(pt_ref, *refs, n_pages, topk):
    page_refs = refs[:IDX_PAGES]
    iqt_ref, w_ref, iknew_ref, bias_ref, biasnew_ref, sc_ref = refs[IDX_PAGES:]
    g = pl.program_id(1)

    @pl.when(g == 0)
    def _():
        sc_ref[...] = jnp.zeros_like(sc_ref)

    iqt = iqt_ref[0]
    w = w_ref[0]
    lane = lax.broadcasted_iota(I32, (PAGE, LANE), 1)

    def col_score(keys_f32):
        s = jnp.dot(keys_f32.astype(BF16), iqt, preferred_element_type=F32)
        return jnp.sum(jnp.maximum(s, 0.0) * w, axis=1, keepdims=True)

    acc = sc_ref[...]
    for j in range(IDX_PAGES):
        col = col_score(page_refs[j][0])
        acc = acc + jnp.where(lane == g * IDX_PAGES + j, col, 0.0)
    sc_ref[...] = acc

    @pl.when(g == n_pages // IDX_PAGES - 1)
    def _():
        past = _order_key(sc_ref[...])
        row = lax.broadcasted_iota(I32, (PAGE, 1), 0)
        new = jnp.where(row == 0, col_score(iknew_ref[0]), -jnp.inf)
        newk = _order_key(new)

        def count_ge(cand):
            hits = jnp.sum(jnp.where(past >= cand, 1, 0).astype(I32), axis=0, keepdims=True)
            hits = jnp.sum(hits, axis=1, keepdims=True)
            return hits + jnp.sum(jnp.where(newk >= cand, 1, 0).astype(I32), axis=0, keepdims=True)

        thr = _kth_largest_key(count_ge, topk, (1, 1))
        thr = jnp.maximum(thr, NEGINF_KEY + 1)
        bias_ref[0] = jnp.where(past >= thr, 0.0, NEG_BIG).astype(F32)
        biasnew_ref[0] = jnp.where(newk >= thr, 0.0, NEG_BIG).astype(F32)


def _dsa_sample_attn_kernel(pt_ref, *refs, n_pages):
    k_refs = refs[:ATT_PAGES]
    v_refs = refs[ATT_PAGES:2 * ATT_PAGES]
    qbd_ref, bias_ref, biasnew_ref, knew_ref, vnew_ref, o_ref, m_ref, l_ref, acc_ref = refs[2 * ATT_PAGES:]
    g = pl.program_id(1)

    @pl.when(g == 0)
    def _():
        m_ref[...] = jnp.full(m_ref.shape, NEG_BIG, F32)
        l_ref[...] = jnp.zeros_like(l_ref)
        acc_ref[...] = jnp.zeros_like(acc_ref)

    qbd = qbd_ref[0]
    scale = A_DH ** -0.5
    lane = lax.broadcasted_iota(I32, (PAGE, LANE), 1)
    tn = (((0,), (0,)), ((), ()))

    def step(k_f32, v_f32, bias_col):
        s = jnp.dot(k_f32.astype(BF16), qbd, preferred_element_type=F32) * scale + bias_col
        m_old = m_ref[...]
        m_new = jnp.maximum(m_old, jnp.max(s, axis=0, keepdims=True))
        p = jnp.exp(s - m_new)
        alpha = jnp.exp(m_old - m_new)
        l_ref[...] = alpha * l_ref[...] + jnp.sum(p, axis=0, keepdims=True)
        pv = lax.dot_general(v_f32.astype(BF16), p.astype(BF16), tn, preferred_element_type=F32)
        acc_ref[...] = alpha * acc_ref[...] + pv
        m_ref[...] = m_new

    bias = bias_ref[0]
    for j in range(ATT_PAGES):
        col = jnp.sum(jnp.where(lane == g * ATT_PAGES + j, bias, 0.0), axis=1, keepdims=True)
        step(k_refs[j][0], v_refs[j][0], col)

    @pl.when(g == n_pages // ATT_PAGES - 1)
    def _():
        step(knew_ref[0], vnew_ref[0], biasnew_ref[0])
        o_ref[0] = acc_ref[...] / l_ref[...]


def _dsa_sample(aq, ak, av, iq, iw, ik, pool_k, pool_v, pool_kidx, page_table):
    bd, n_pages = page_table.shape
    n_pool = pool_k.shape[0]
    topk = min(TOPK_MAX, (n_pages * PAGE + 1) // 4)
    assert n_pages == LANE and n_pages % IDX_PAGES == 0 and n_pages % ATT_PAGES == 0
    kvw = A_KV * A_DH
    iqt = jnp.zeros((bd, I_DIM, LANE), BF16).at[:, :, :I_HEADS].set(
        iq.reshape(bd, I_HEADS, I_DIM).transpose(0, 2, 1))
    w_row = jnp.where(jnp.arange(LANE) < I_HEADS, iw, 0.0).reshape(bd, 1, LANE)
    pad_rows = lambda a: jnp.zeros((bd, PAGE, a.shape[-1]), F32).at[:, 0, :].set(a)
    ik_new, k_new, v_new = pad_rows(ik), pad_rows(ak), pad_rows(av)
    q4 = aq.reshape(bd, A_KV, A_GROUP, A_DH)
    qbd = jnp.zeros((bd, A_KV, A_DH, LANE), BF16)
    for n in range(A_KV):
        qbd = qbd.at[:, n, :, n * A_GROUP:(n + 1) * A_GROUP].set(q4[:, n].transpose(0, 2, 1))
    qbd = qbd.reshape(bd, kvw, LANE)

    def page_spec(width, j, per):
        return pl.BlockSpec((1, PAGE, width), lambda b, g, pt: (pt[b, g * per + j], 0, 0))

    per_row = lambda shape: pl.BlockSpec((1,) + shape, lambda b, g, pt: (b, 0, 0))
    bias, bias_new = pl.pallas_call(
        functools.partial(_dsa_sample_index_kernel, n_pages=n_pages, topk=topk),
        grid_spec=pltpu.PrefetchScalarGridSpec(
            num_scalar_prefetch=1,
            grid=(bd, n_pages // IDX_PAGES),
            in_specs=[page_spec(I_DIM, j, IDX_PAGES) for j in range(IDX_PAGES)]
            + [per_row((I_DIM, LANE)), per_row((1, LANE)), per_row((PAGE, I_DIM))],
            out_specs=[per_row((PAGE, LANE)), per_row((PAGE, 1))],
            scratch_shapes=[pltpu.VMEM((PAGE, LANE), F32)],
        ),
        out_shape=[jax.ShapeDtypeStruct((bd, PAGE, LANE), F32), jax.ShapeDtypeStruct((bd, PAGE, 1), F32)],
        compiler_params=_cparams(("parallel", "arbitrary")),
        name="dsa_sample_index",
    )(page_table, *([pool_kidx] * IDX_PAGES), iqt, w_row, ik_new)

    pk = pool_k.reshape(n_pool, PAGE, kvw)
    pv = pool_v.reshape(n_pool, PAGE, kvw)
    o_t = pl.pallas_call(
        functools.partial(_dsa_sample_attn_kernel, n_pages=n_pages),
        grid_spec=pltpu.PrefetchScalarGridSpec(
            num_scalar_prefetch=1,
            grid=(bd, n_pages // ATT_PAGES),
            in_specs=[page_spec(kvw, j, ATT_PAGES) for j in range(ATT_PAGES)] * 2
            + [per_row((kvw, LANE)), per_row((PAGE, LANE)), per_row((PAGE, 1)),
               per_row((PAGE, kvw)), per_row((PAGE, kvw))],
            out_specs=per_row((kvw, LANE)),
            scratch_shapes=[pltpu.VMEM((1, LANE), F32), pltpu.VMEM((1, LANE), F32), pltpu.VMEM((kvw, LANE), F32)],
        ),
        out_shape=jax.ShapeDtypeStruct((bd, kvw, LANE), F32),
        compiler_params=_cparams(("parallel", "arbitrary")),
        name="dsa_sample_attn",
    )(page_table, *([pk] * ATT_PAGES), *([pv] * ATT_PAGES), qbd, bias, bias_new, k_new, v_new)
    o4 = o_t.reshape(bd, A_KV, A_DH, LANE)
    att = jnp.stack([o4[:, n, :, n * A_GROUP:(n + 1) * A_GROUP] for n in range(A_KV)], axis=1)
    return att.transpose(0, 1, 3, 2).reshape(bd, A_HEADS * A_DH).astype(BF16)


def _gate_mix_kernel(rn_ref, att_ref, wr_ref, wa_ref, gr_ref, ga_ref, o_ref):
    r = jnp.dot(rn_ref[...], wr_ref[...], preferred_element_type=F32)
    a = jnp.dot(att_ref[...], wa_ref[...], preferred_element_type=F32)
    o_ref[...] = (gr_ref[...].astype(F32) * r + ga_ref[...].astype(F32) * a).astype(o_ref.dtype)


def _out_proj_kernel(x_ref, mix_ref, wo_ref, nw_ref, h_ref, hn_ref):
    h = x_ref[...] + jnp.dot(mix_ref[...], wo_ref[...], preferred_element_type=F32)
    h_ref[...] = h
    ms = jnp.mean(h * h, axis=-1, keepdims=True)
    hn_ref[...] = (h * lax.rsqrt(ms + EPS) * nw_ref[...]).astype(hn_ref.dtype)


def _mixer_out(x2d, rn, att, gates, wr, wa, wo, norm2_w, tm):
    n, d = x2d.shape
    tn = 512
    ncol = d // tn
    mixed = pl.pallas_call(
        _gate_mix_kernel,
        grid=(ncol, n // tm),
        in_specs=[
            pl.BlockSpec((tm, rn.shape[1]), lambda j, i: (i, 0)),
            pl.BlockSpec((tm, att.shape[1]), lambda j, i: (i, 0)),
            pl.BlockSpec((wr.shape[0], tn), lambda j, i: (0, j)),
            pl.BlockSpec((wa.shape[0], tn), lambda j, i: (0, j)),
            pl.BlockSpec((tm, tn), lambda j, i: (i, j)),
            pl.BlockSpec((tm, tn), lambda j, i: (i, ncol + j)),
        ],
        out_specs=pl.BlockSpec((tm, tn), lambda j, i: (i, j)),
        out_shape=jax.ShapeDtypeStruct((n, d), BF16),
        compiler_params=_cparams(("parallel", "parallel")),
        name="gate_mix",
    )(rn, att, wr, wa, gates, gates)
    tm2 = min(tm, 256)
    h, hn = pl.pallas_call(
        _out_proj_kernel,
        grid=(n // tm2,),
        in_specs=[
            pl.BlockSpec((tm2, d), lambda i: (i, 0)),
            pl.BlockSpec((tm2, d), lambda i: (i, 0)),
            pl.BlockSpec((d, d), lambda i: (0, 0)),
            pl.BlockSpec((1, d), lambda i: (0, 0)),
        ],
        out_specs=[pl.BlockSpec((tm2, d), lambda i: (i, 0)), pl.BlockSpec((tm2, d), lambda i: (i, 0))],
        out_shape=[jax.ShapeDtypeStruct((n, d), F32), jax.ShapeDtypeStruct((n, d), BF16)],
        compiler_params=_cparams(("parallel",)),
        name="out_proj",
    )(x2d, mixed, wo, norm2_w.reshape(1, d))
    return h, hn


def _top_desc(x, count):
    vals = []
    cur = x
    for _ in range(count):
        mx = jnp.max(cur, axis=0, keepdims=True)
        vals.append(mx)
        cur = jnp.where(cur == mx, -jnp.inf, cur)
    return vals


def _peer_route_kernel(hn_ref, wq_ref, k1_ref, k2_ref, s1_ref, e1_ref, s2_ref, e2_ref, thr_ref):
    q = jnp.dot(hn_ref[...], wq_ref[...], preferred_element_type=F32).astype(BF16)
    nt = (((1,), (1,)), ((), ()))
    for h in range(P_HEADS):
        qh = q[:, h * P_DKEY:(h + 1) * P_DKEY]
        s1 = lax.dot_general(k1_ref[h], qh, nt, preferred_element_type=F32)
        s2 = lax.dot_general(k2_ref[h], qh, nt, preferred_element_type=F32)
        v1 = _top_desc(s1, P_TOPK)
        v2 = _top_desc(s2, P_TOPK)
        v2blk = jnp.concatenate(v2, axis=0)
        cand = jnp.concatenate([v1[i] + v2blk for i in range(P_TOPK)], axis=0)
        c = _top_desc(cand, P_TOPK)
        z = jnp.zeros_like(c[0])
        for i in range(P_TOPK):
            z = z + jnp.exp(c[i] - c[0])
        s1m = jnp.where(s1 >= v1[-1], s1, -jnp.inf)
        s2m = jnp.where(s2 >= v2[-1], s2, -jnp.inf)
        s1_ref[h] = s1m
        s2_ref[h] = s2m
        e1_ref[h] = jnp.exp(s1m - v1[0]) / z
        e2_ref[h] = jnp.exp(s2m - v2[0])
        thr_ref[h] = jnp.broadcast_to(c[-1], thr_ref.shape[1:])


def _peer_expert_kernel(hn_ref, h_ref, u_ref, vt_ref, s1_ref, e1_ref, s2_ref, e2_ref, thr_ref,
                        y_ref, acc_ref, *, a_per_tile):
    j = pl.program_id(1)

    @pl.when(j == 0)
    def _():
        acc_ref[...] = jnp.zeros_like(acc_ref)

    nt = (((1,), (1,)), ((), ()))
    x = lax.dot_general(u_ref[...], hn_ref[...], nt, preferred_element_type=F32)
    act = 0.5 * x * (1.0 + lax.erf(x * (2.0 ** -0.5)))
    parts = []
    for al in range(a_per_tile):
        a = j * a_per_tile + al
        w = jnp.zeros((P_NKEYS, act.shape[1]), F32)
        for h in range(P_HEADS):
            s1row = s1_ref[h, pl.ds(a, 1), :]
            e1row = e1_ref[h, pl.ds(a, 1), :]
            hit = (s1row + s2_ref[h]) >= thr_ref[h, 0:1, :]
            w = w + jnp.where(hit, e2_ref[h] * e1row, 0.0)
        parts.append((w * act[al * P_NKEYS:(al + 1) * P_NKEYS]).astype(BF16))
    p = jnp.concatenate(parts, axis=0) if a_per_tile > 1 else parts[0]
    acc_ref[...] += jnp.dot(vt_ref[...], p, preferred_element_type=F32)

    @pl.when(j == pl.num_programs(1) - 1)
    def _():
        y_ref[...] = h_ref[...] + acc_ref[...].T


def _peer(hn, h, wq, k1p, k2p, u_bf, vt_bf, tm, te):
    n, d = hn.shape
    hk = P_HEADS * P_DKEY
    tok = lambda i: (0, 0, i)
    head_blk = pl.BlockSpec((P_HEADS, P_NKEYS, tm), tok)
    s1, e1, s2, e2, thr = pl.pallas_call(
        _peer_route_kernel,
        grid=(n // tm,),
        in_specs=[
            pl.BlockSpec((tm, d), lambda i: (i, 0)),
            pl.BlockSpec((d, hk), lambda i: (0, 0)),
            pl.BlockSpec((P_HEADS, P_NKEYS, P_DKEY), lambda i: (0, 0, 0)),
            pl.BlockSpec((P_HEADS, P_NKEYS, P_DKEY), lambda i: (0, 0, 0)),
        ],
        out_specs=[head_blk] * 4 + [pl.BlockSpec((P_HEADS, 8, tm), tok)],
        out_shape=[jax.ShapeDtypeStruct((P_HEADS, P_NKEYS, n), F32)] * 4
        + [jax.ShapeDtypeStruct((P_HEADS, 8, n), F32)],
        compiler_params=_cparams(("parallel",)),
        name="peer_route",
    )(hn, wq, k1p, k2p)

    n_exp = u_bf.shape[0]
    tok2 = lambda i, j: (0, 0, i)
    head_blk2 = pl.BlockSpec((P_HEADS, P_NKEYS, tm), tok2)
    return pl.pallas_call(
        functools.partial(_peer_expert_kernel, a_per_tile=te // P_NKEYS),
        grid=(n // tm, n_exp // te),
        in_specs=[
            pl.BlockSpec((tm, d), lambda i, j: (i, 0)),
            pl.BlockSpec((tm, d), lambda i, j: (i, 0)),
            pl.BlockSpec((te, d), lambda i, j: (j, 0)),
            pl.BlockSpec((d, te), lambda i, j: (0, j)),
            head_blk2, head_blk2, head_blk2, head_blk2,
            pl.BlockSpec((P_HEADS, 8, tm), tok2),
        ],
        out_specs=pl.BlockSpec((tm, d), lambda i, j: (i, 0)),
        out_shape=jax.ShapeDtypeStruct((n, d), F32),
        scratch_shapes=[pltpu.VMEM((d, tm), F32)],
        compiler_params=_cparams(("parallel", "arbitrary")),
        name="peer_experts",
    )(hn, h, u_bf, vt_bf, s1, e1, s2, e2, thr)


def _prep_weights(w_in, w_ret_o, w_att_o, w_out, peer_wq, peer_keys1, peer_keys2, peer_u, peer_v):
    iw_pad = jnp.pad(w_in[:, IW_COL:GATE_COL], ((0, 0), (0, 3 * LANE - I_HEADS)))
    w_bf = jnp.concatenate([w_in[:, :IW_COL], iw_pad, w_in[:, GATE_COL:]], axis=1).astype(BF16)
    half = P_DKEY // 2
    k1p = jnp.pad(peer_keys1, ((0, 0), (0, 0), (0, half))).astype(BF16)
    k2p = jnp.pad(peer_keys2, ((0, 0), (0, 0), (half, 0))).astype(BF16)
    return dict(w_bf=w_bf, wr=w_ret_o.astype(BF16), wa=w_att_o.astype(BF16), wo=w_out.astype(BF16),
                wq=peer_wq.astype(BF16), k1p=k1p, k2p=k2p,
                u_bf=peer_u.astype(BF16), vt_bf=peer_v.T.astype(BF16))


def _prompt_layer(x, W, norm1_w, q_norm_w, k_norm_w, ret_gn_w, norm2_w):
    bsz, seq, d = x.shape
    x2d = x.reshape(bsz * seq, d)
    tm = 512
    p = _mixer_in(x2d, jnp.arange(seq), norm1_w, W["w_bf"], q_norm_w, k_norm_w, tm)
    s0 = jnp.zeros((bsz, R_HEADS, R_DK, R_DV), F32)
    rn, S = _retention_prompt(p["rq"], p["rk"], p["rvg"], ret_gn_w, s0, bsz, seq)
    att = _dsa_prompt(p["iq"], p["iw"], p["ik_bf"], p["aq"], p["ak_bf"], p["av_bf"], bsz, seq)
    h, hn = _mixer_out(x2d, rn, att, p["gates"], W["wr"], W["wa"], W["wo"], norm2_w, tm)
    y = _peer(hn, h, W["wq"], W["k1p"], W["k2p"], W["u_bf"], W["vt_bf"], tm=256, te=512)
    return (y.reshape(bsz, seq, d), p["ak"].reshape(bsz, seq, A_KV, A_DH), p["av"].reshape(bsz, seq, A_KV, A_DH),
            p["ik"].reshape(bsz, seq, I_DIM), S)


def _sample_layer(x, pool_k, pool_v, pool_kidx, s_in, page_table, W, norm1_w, q_norm_w, k_norm_w,
                  ret_gn_w, norm2_w):
    bd, ts, d = x.shape
    assert ts == 1
    x2d = x.reshape(bd, d)
    past = page_table.shape[1] * PAGE
    p = _mixer_in(x2d, jnp.full((bd,), past, I32), norm1_w, W["w_bf"], q_norm_w, k_norm_w, bd)
    rn, S = _retention_sample(p["rq"], p["rk"], p["rvg"], ret_gn_w, s_in)
    att = _dsa_sample(p["aq"], p["ak"], p["av"], p["iq"], p["iw"], p["ik"], pool_k, pool_v, pool_kidx, page_table)
    h, hn = _mixer_out(x2d, rn, att, p["gates"], W["wr"], W["wa"], W["wo"], norm2_w, bd)
    pad = LANE - bd
    hn_p = jnp.pad(hn, ((0, pad), (0, 0)))
    h_p = jnp.pad(h, ((0, pad), (0, 0)))
    y = _peer(hn_p, h_p, W["wq"], W["k1p"], W["k2p"], W["u_bf"], W["vt_bf"], tm=LANE, te=512)[:bd]
    return (y.reshape(bd, ts, d), p["ak"].reshape(bd, ts, A_KV, A_DH), p["av"].reshape(bd, ts, A_KV, A_DH),
            p["ik"].reshape(bd, ts, I_DIM), S)


def kernel(x_prompt, x_sample, cache_k, cache_v, cache_k_idx, state_ret, page_table, norm1_w, w_in, q_norm_w,
           k_norm_w, ret_gn_w, w_ret_o, w_att_o, w_out, norm2_w, peer_wq, peer_keys1, peer_keys2, peer_u, peer_v):
    depth = w_in.shape[0]
    assert depth == 1
    l = 0
    W = _prep_weights(w_in[l], w_ret_o[l], w_att_o[l], w_out[l], peer_wq[l], peer_keys1[l], peer_keys2[l],
                      peer_u[l], peer_v[l])
    yp, kp, vp, ikp, sp = _prompt_layer(x_prompt, W, norm1_w[l], q_norm_w[l], k_norm_w[l], ret_gn_w[l], norm2_w[l])
    ys, kn, vn, ikn, sn = _sample_layer(x_sample, cache_k[l], cache_v[l], cache_k_idx[l], state_ret[l], page_table,
                                        W, norm1_w[l], q_norm_w[l], k_norm_w[l], ret_gn_w[l], norm2_w[l])
    return (yp, ys, kp[None], vp[None], ikp[None], sp[None], kn[None], vn[None], ikn[None], sn[None])
```

```python
import functools

import jax
import jax.numpy as jnp
from jax import lax
from jax.experimental import pallas as pl
from jax.experimental.pallas import tpu as pltpu

F32 = jnp.float32
BF16 = jnp.bfloat16
I32 = jnp.int32

D_MODEL = 2048
PAGE = 128
R_HEADS, R_DK, R_DV, R_CHUNK = 8, 128, 256, 128
A_HEADS, A_KV, A_DH = 16, 4, 128
A_GROUP = A_HEADS // A_KV
I_HEADS, I_DIM = 16, 128
TOPK_MAX = 256
Q_BLOCK = 128
ROPE_THETA = 10000.0
P_HEADS, P_NKEYS, P_DKEY, P_TOPK = 8, 128, 128, 16
EPS = 1e-6

LANE = 128
VMEM_LIMIT = 56 * 1024 * 1024
NEG_BIG = -1e30
LOG2E = 1.4426950408889634
INT_MIN = -2147483648
NEGINF_KEY = INT_MIN + 0x7FFFFF

SEG = dict(rq=(0, 8), rk=(8, 8), rvg=(16, 32), aq=(48, 16), ak=(64, 4), av=(68, 4),
           iq=(72, 16), ik=(88, 1), iw=(89, 1), gates=(92, 32))
IW_COL = 11392
GATE_COL = 11408


def _cparams(sem):
    return pltpu.CompilerParams(dimension_semantics=sem, vmem_limit_bytes=VMEM_LIMIT)


def _rmsnorm_kernel(x_ref, w_ref, o_ref):
    x = x_ref[...]
    ms = jnp.mean(x * x, axis=-1, keepdims=True)
    o_ref[...] = (x * lax.rsqrt(ms + EPS) * w_ref[...]).astype(o_ref.dtype)


def _rmsnorm(x, w, tm):
    n, d = x.shape
    return pl.pallas_call(
        _rmsnorm_kernel,
        grid=(n // tm,),
        in_specs=[pl.BlockSpec((tm, d), lambda i: (i, 0)), pl.BlockSpec((1, d), lambda i: (0, 0))],
        out_specs=pl.BlockSpec((tm, d), lambda i: (i, 0)),
        out_shape=jax.ShapeDtypeStruct((n, d), BF16),
        compiler_params=_cparams(("parallel",)),
        name="rmsnorm",
    )(x, w.reshape(1, d))


def _proj_kernel(x_ref, w_ref, cos_ref, sin_ref, nw_ref, *out_refs, mode, scale, heads):
    z = jnp.dot(x_ref[...], w_ref[...], preferred_element_type=F32)
    if mode in ("rope", "norm_rope"):
        cos, sin = cos_ref[...], sin_ref[...]
        parts = []
        for h in range(heads):
            zh = z[:, h * LANE:(h + 1) * LANE]
            if mode == "norm_rope":
                zh = zh * lax.rsqrt(jnp.mean(zh * zh, axis=-1, keepdims=True) + EPS) * nw_ref[...]
            zh = zh * cos + pltpu.roll(zh, LANE // 2, 1) * sin
            if scale != 1.0:
                zh = zh * scale
            parts.append(zh)
        z = jnp.concatenate(parts, axis=1) if heads > 1 else parts[0]
    elif mode == "sigmoid":
        z = jax.nn.sigmoid(z)
    elif scale != 1.0:
        z = z * scale
    for o in out_refs:
        o[...] = z.astype(o.dtype)


def _proj(xn, w, seg, cos2, sin2, nw, *, mode, scale, out_dtypes, tm, tn):
    n, d = xn.shape
    c0, nblk = seg
    width = nblk * LANE
    tn = min(tn, width)
    c0t = (c0 * LANE) // tn
    assert (c0 * LANE) % tn == 0 and width % tn == 0 and n % tm == 0
    pos_blocks = cos2.shape[0] // tm
    kern = functools.partial(_proj_kernel, mode=mode, scale=scale, heads=tn // LANE)
    outs = pl.pallas_call(
        kern,
        grid=(width // tn, n // tm),
        in_specs=[
            pl.BlockSpec((tm, d), lambda j, i: (i, 0)),
            pl.BlockSpec((d, tn), lambda j, i: (0, c0t + j)),
            pl.BlockSpec((tm, LANE), lambda j, i: (i % pos_blocks, 0)),
            pl.BlockSpec((tm, LANE), lambda j, i: (i % pos_blocks, 0)),
            pl.BlockSpec((1, LANE), lambda j, i: (0, 0)),
        ],
        out_specs=[pl.BlockSpec((tm, tn), lambda j, i: (i, j)) for _ in out_dtypes],
        out_shape=[jax.ShapeDtypeStruct((n, width), dt) for dt in out_dtypes],
        compiler_params=_cparams(("parallel", "parallel")),
        name="proj_" + mode,
    )(xn, w, cos2, sin2, nw)
    return outs


def _rope_tables(pos):
    half = LANE // 2
    freqs = ROPE_THETA ** (-jnp.arange(half, dtype=F32) / half)
    ang = pos.astype(F32)[:, None] * freqs[None, :]
    cos, sin = jnp.cos(ang), jnp.sin(ang)
    return jnp.concatenate([cos, cos], axis=1), jnp.concatenate([-sin, sin], axis=1)


def _mixer_in(x2d, pos_rows, norm1_w, w_bf, q_norm_w, k_norm_w, tm):
    xn = _rmsnorm(x2d, norm1_w, tm)
    cos2, sin2 = _rope_tables(pos_rows)
    ones = jnp.ones((1, LANE), F32)
    p = functools.partial(_proj, xn, w_bf, cos2=cos2, sin2=sin2, tm=tm, tn=512)
    out = {}
    out["rq"], = p(seg=SEG["rq"], nw=ones, mode="rope", scale=1.0, out_dtypes=(BF16,))
    out["rk"], = p(seg=SEG["rk"], nw=ones, mode="rope", scale=R_DK ** -0.5, out_dtypes=(BF16,))
    out["rvg"], = p(seg=SEG["rvg"], nw=ones, mode="plain", scale=1.0, out_dtypes=(BF16,))
    out["aq"], = p(seg=SEG["aq"], nw=q_norm_w.reshape(1, LANE), mode="norm_rope", scale=A_DH ** -0.5 * LOG2E,
                   out_dtypes=(BF16,))
    out["ak"], out["ak_bf"] = p(seg=SEG["ak"], nw=k_norm_w.reshape(1, LANE), mode="norm_rope", scale=1.0,
                                 out_dtypes=(F32, BF16))
    out["av"], out["av_bf"] = p(seg=SEG["av"], nw=ones, mode="plain", scale=1.0, out_dtypes=(F32, BF16))
    out["iq"], = p(seg=SEG["iq"], nw=ones, mode="rope", scale=I_DIM ** -0.5, out_dtypes=(BF16,))
    out["ik"], out["ik_bf"] = p(seg=SEG["ik"], nw=ones, mode="rope", scale=1.0, out_dtypes=(F32, BF16))
    out["iw"], = p(seg=SEG["iw"], nw=ones, mode="plain", scale=I_HEADS ** -0.5, out_dtypes=(F32,))
    out["gates"], = p(seg=SEG["gates"], nw=ones, mode="sigmoid", scale=1.0, out_dtypes=(BF16,))
    return out


def _ret_tables(chunk):
    log_g = jnp.log1p(-jnp.exp2(-5.0 - jnp.arange(R_HEADS, dtype=F32)))
    n = jnp.arange(chunk, dtype=F32)
    diff = n[:, None] - n[None, :]
    causal = diff >= 0
    dmat = jnp.where(causal, jnp.exp(jnp.where(causal, diff, 0.0)[None] * log_g[:, None, None]), 0.0)
    qdec = jnp.exp((n + 1.0)[None, :] * log_g[:, None])
    kdec = jnp.exp((chunk - 1.0 - n)[None, :] * log_g[:, None])
    gc = jnp.exp(chunk * log_g)
    return dmat, qdec, kdec, gc


def _ret_kernel(q_ref, k_ref, v_ref, g_ref, gnw_ref, dmat_ref, qdec_ref, kdec_ref, gc_ref, s0_ref,
                o_ref, s_ref):
    c = pl.program_id(1)

    @pl.when(c == 0)
    def _():
        s_ref[...] = s0_ref[...]

    nt = (((1,), (1,)), ((), ()))
    tn = (((0,), (0,)), ((), ()))
    for h in range(R_HEADS):
        S = s_ref[0, h]
        q = q_ref[:, h * R_DK:(h + 1) * R_DK]
        k = k_ref[:, h * R_DK:(h + 1) * R_DK]
        v = v_ref[:, h * R_DV:(h + 1) * R_DV]
        scores = lax.dot_general(q, k, nt, preferred_element_type=F32) * dmat_ref[h]
        intra = jnp.dot(scores.astype(BF16), v, preferred_element_type=F32)
        qd = (q.astype(F32) * qdec_ref[h]).astype(BF16)
        cross = jnp.dot(qd, S.astype(BF16), preferred_element_type=F32)
        kd = (k.astype(F32) * kdec_ref[h]).astype(BF16)
        s_ref[0, h] = gc_ref[h] * S + lax.dot_general(kd, v, tn, preferred_element_type=F32)
        o = intra + cross
        mu = jnp.mean(o, axis=-1, keepdims=True)
        var = jnp.mean(jnp.square(o - mu), axis=-1, keepdims=True)
        g = g_ref[:, h * R_DV:(h + 1) * R_DV].astype(F32)
        rn = (o - mu) * lax.rsqrt(var + EPS) * gnw_ref[:, h * R_DV:(h + 1) * R_DV] * (g * jax.nn.sigmoid(g))
        o_ref[:, h * R_DV:(h + 1) * R_DV] = rn.astype(o_ref.dtype)


def _retention_prompt(rq, rk, rvg, gn_w, s0, bsz, seq):
    C = R_CHUNK
    nc = seq // C
    dmat, qdec, kdec, gc = _ret_tables(C)
    qdec = jnp.broadcast_to(qdec[:, :, None], (R_HEADS, C, R_DK))
    kdec = jnp.broadcast_to(kdec[:, :, None], (R_HEADS, C, R_DK))
    gc = jnp.broadcast_to(gc[:, None, None], (R_HEADS, 1, R_DV))
    hk, hv = R_HEADS * R_DK, R_HEADS * R_DV
    row = lambda b, c: (b * nc + c, 0)
    const3 = lambda b, c: (0, 0, 0)
    state = pl.BlockSpec((1, R_HEADS, R_DK, R_DV), lambda b, c: (b, 0, 0, 0))
    rn, S = pl.pallas_call(
        _ret_kernel,
        grid=(bsz, nc),
        in_specs=[
            pl.BlockSpec((C, hk), row),
            pl.BlockSpec((C, hk), row),
            pl.BlockSpec((C, hv), row),
            pl.BlockSpec((C, hv), lambda b, c: (b * nc + c, 1)),
            pl.BlockSpec((1, hv), lambda b, c: (0, 0)),
            pl.BlockSpec((R_HEADS, C, C), const3),
            pl.BlockSpec((R_HEADS, C, R_DK), const3),
            pl.BlockSpec((R_HEADS, C, R_DK), const3),
            pl.BlockSpec((R_HEADS, 1, R_DV), const3),
            state,
        ],
        out_specs=[pl.BlockSpec((C, hv), row), state],
        out_shape=[jax.ShapeDtypeStruct((bsz * seq, hv), BF16),
                   jax.ShapeDtypeStruct((bsz, R_HEADS, R_DK, R_DV), F32)],
        compiler_params=_cparams(("parallel", "arbitrary")),
        name="retention",
    )(rq, rk, rvg, rvg, gn_w.reshape(1, -1), dmat, qdec, kdec, gc, s0)
    return rn, S


def _ret_step_kernel(qc_ref, kc_ref, v_ref, g_ref, gnw_ref, gam_ref, s0_ref, o_ref, s_ref):
    for h in range(R_HEADS):
        S = s0_ref[0, h]
        qc, kc = qc_ref[0, h], kc_ref[0, h]
        v = v_ref[0, h]
        gam = gam_ref[h]
        qk = jnp.sum(qc * kc, axis=0, keepdims=True)
        cross = jnp.sum((qc * gam) * S, axis=0, keepdims=True)
        o = qk * v + cross
        s_ref[0, h] = gam * S + kc * v
        mu = jnp.mean(o, axis=-1, keepdims=True)
        var = jnp.mean(jnp.square(o - mu), axis=-1, keepdims=True)
        g = g_ref[0, h]
        o_ref[0, h] = (o - mu) * lax.rsqrt(var + EPS) * gnw_ref[0, h] * (g * jax.nn.sigmoid(g))


def _retention_sample(rq, rk, rvg, gn_w, s0):
    bd = rq.shape[0]
    _, qdec, _, _ = _ret_tables(1)
    gam = jnp.broadcast_to(qdec[:, :, None], (R_HEADS, 1, R_DV)).astype(F32)
    qc = rq.astype(F32).reshape(bd, R_HEADS, R_DK, 1)
    kc = rk.astype(F32).reshape(bd, R_HEADS, R_DK, 1)
    rv = rvg[:, :R_HEADS * R_DV].astype(F32).reshape(bd, R_HEADS, 1, R_DV)
    rg = rvg[:, R_HEADS * R_DV:].astype(F32).reshape(bd, R_HEADS, 1, R_DV)
    gnw = gn_w.reshape(1, R_HEADS, 1, R_DV)
    col = pl.BlockSpec((1, R_HEADS, R_DK, 1), lambda b: (b, 0, 0, 0))
    rowv = pl.BlockSpec((1, R_HEADS, 1, R_DV), lambda b: (b, 0, 0, 0))
    st = pl.BlockSpec((1, R_HEADS, R_DK, R_DV), lambda b: (b, 0, 0, 0))
    rn, S = pl.pallas_call(
        _ret_step_kernel,
        grid=(bd,),
        in_specs=[col, col, rowv, rowv,
                  pl.BlockSpec((1, R_HEADS, 1, R_DV), lambda b: (0, 0, 0, 0)),
                  pl.BlockSpec((R_HEADS, 1, R_DV), lambda b: (0, 0, 0)), st],
        out_specs=[rowv, st],
        out_shape=[jax.ShapeDtypeStruct((bd, R_HEADS, 1, R_DV), F32),
                   jax.ShapeDtypeStruct((bd, R_HEADS, R_DK, R_DV), F32)],
        compiler_params=_cparams(("parallel",)),
        name="retention_step",
    )(qc, kc, rv, rg, gnw, gam, s0)
    return rn.reshape(bd, R_HEADS * R_DV).astype(BF16), S


def _order_key(x):
    bits = pltpu.bitcast(x, I32)
    return bits ^ ((bits >> 31) & 0x7FFFFFFF)


def _kth_largest_key(count_ge, k, shape):
    cur = jnp.where(count_ge(jnp.zeros(shape, I32)) >= k, 0, INT_MIN).astype(I32)

    def body(i, cur):
        cand = cur | jnp.left_shift(jnp.int32(1), 30 - i)
        return jnp.where(count_ge(cand) >= k, cand, cur)

    return lax.fori_loop(0, 31, body, cur)


KEY_UNIT = 512
TRIP_KEYS = 1024


def _dsa_prompt_kernel(iq_ref, iw_ref, ik_ref, aq_ref, ak_ref, av_ref, o_ref,
                       keys_ref, bias_ref, m_ref, acc_ref, *, topk):
    qb = pl.program_id(1)
    ntrips = (qb * Q_BLOCK + Q_BLOCK + TRIP_KEYS - 1) // TRIP_KEYS
    nt = (((1,), (1,)), ((), ()))

    iq = iq_ref[...]
    pair = lambda hp: jnp.concatenate([iq[:, (2 * hp) * LANE:(2 * hp + 1) * LANE],
                                       iq[:, (2 * hp + 1) * LANE:(2 * hp + 2) * LANE]], axis=0)
    rhs = [pair(hp) for hp in range(I_HEADS // 2)]
    w_t = iw_ref[...].T
    t_idx = qb * Q_BLOCK + lax.broadcasted_iota(I32, (Q_BLOCK, Q_BLOCK), 1)
    row_iota = lax.broadcasted_iota(I32, (Q_BLOCK, Q_BLOCK), 0)

    def score_trip(t, carry):
        for c in range(TRIP_KEYS // Q_BLOCK):
            off = pl.multiple_of(t * TRIP_KEYS + c * Q_BLOCK, Q_BLOCK)
            ikc = ik_ref[0, pl.ds(off, Q_BLOCK), :]
            sc = jnp.zeros((Q_BLOCK, Q_BLOCK), F32)
            for hp in range(I_HEADS // 2):
                prod = lax.dot_general(ikc, rhs[hp], nt, preferred_element_type=F32)
                sc = sc + (jnp.maximum(prod[:, :Q_BLOCK], 0.0) * w_t[2 * hp:2 * hp + 1, :]
                           + jnp.maximum(prod[:, Q_BLOCK:], 0.0) * w_t[2 * hp + 1:2 * hp + 2, :])
            sc = jnp.where(off + row_iota <= t_idx, sc, -jnp.inf)
            keys_ref[pl.ds(off, Q_BLOCK), :] = _order_key(sc)
        return carry

    lax.fori_loop(0, ntrips, score_trip, 0)

    def count_ge(cand):
        def body(t, acc):
            for c in range(TRIP_KEYS // KEY_UNIT):
                off = pl.multiple_of(t * TRIP_KEYS + c * KEY_UNIT, KEY_UNIT)
                blk = keys_ref[pl.ds(off, KEY_UNIT), :]
                hit = jnp.where(blk >= cand, 1, 0).astype(I32)
                acc = acc + jnp.sum(hit.reshape(KEY_UNIT // 8, 8, Q_BLOCK), axis=0)
            return acc
        acc = lax.fori_loop(0, ntrips, body, jnp.zeros((8, Q_BLOCK), I32))
        return jnp.sum(acc, axis=0, keepdims=True)

    thr = _kth_largest_key(count_ge, topk, (1, Q_BLOCK))
    thr = jnp.maximum(thr, NEGINF_KEY + 1)

    def bias_trip(t, carry):
        for c in range(TRIP_KEYS // Q_BLOCK):
            off = pl.multiple_of(t * TRIP_KEYS + c * Q_BLOCK, Q_BLOCK)
            sel = keys_ref[pl.ds(off, Q_BLOCK), :] >= thr
            bias_ref[:, pl.ds(off, Q_BLOCK)] = jnp.where(sel, 0.0, NEG_BIG).astype(F32).T
        return carry

    lax.fori_loop(0, ntrips, bias_trip, 0)

    aq = aq_ref[...]
    rows = A_GROUP * Q_BLOCK
    ones_v = jnp.ones((KEY_UNIT, A_DH), BF16)
    subs = TRIP_KEYS // KEY_UNIT
    for n in range(A_KV):
        qg = jnp.concatenate([aq[:, (n * A_GROUP + g) * A_DH:(n * A_GROUP + g + 1) * A_DH]
                              for g in range(A_GROUP)], axis=0)

        def logits(off, n=n, qg=qg):
            kch = ak_ref[0, pl.ds(off, KEY_UNIT), n * A_DH:(n + 1) * A_DH]
            b = bias_ref[:, pl.ds(off, KEY_UNIT)]
            s = lax.dot_general(qg, kch, nt, preferred_element_type=F32)
            return s + jnp.concatenate([b] * A_GROUP, axis=0)

        m_ref[...] = jnp.full((rows, LANE), NEG_BIG, F32)

        def row_max(t, carry, logits=logits):
            m = m_ref[...]
            for c in range(subs):
                off = pl.multiple_of(t * TRIP_KEYS + c * KEY_UNIT, KEY_UNIT)
                m = jnp.maximum(m, jnp.max(logits(off), axis=-1, keepdims=True))
            m_ref[...] = m
            return carry

        lax.fori_loop(0, ntrips, row_max, 0)
        acc_ref[...] = jnp.zeros((rows, 2 * A_DH), F32)

        def weigh(t, carry, n=n, logits=logits):
            m = jnp.concatenate([m_ref[...]] * (KEY_UNIT // LANE), axis=1)
            pv = jnp.zeros((rows, 2 * A_DH), F32)
            for c in range(subs):
                off = pl.multiple_of(t * TRIP_KEYS + c * KEY_UNIT, KEY_UNIT)
                p = jnp.exp2(logits(off) - m).astype(BF16)
                vch = av_ref[0, pl.ds(off, KEY_UNIT), n * A_DH:(n + 1) * A_DH]
                pv = pv + jnp.dot(p, jnp.concatenate([vch, ones_v], axis=1), preferred_element_type=F32)
            acc_ref[...] += pv
            return carry

        lax.fori_loop(0, ntrips, weigh, 0)
        acc = acc_ref[...]
        o = acc[:, :A_DH] / acc[:, A_DH:]
        for g in range(A_GROUP):
            hh = n * A_GROUP + g
            o_ref[:, hh * A_DH:(hh + 1) * A_DH] = o[g * Q_BLOCK:(g + 1) * Q_BLOCK].astype(o_ref.dtype)


def _dsa_prompt(iq, iw, ik_bf, aq, ak_bf, av_bf, bsz, seq):
    nb = seq // Q_BLOCK
    topk = min(TOPK_MAX, seq // 4)
    assert seq % TRIP_KEYS == 0
    ik3 = ik_bf.reshape(bsz, seq, I_DIM)
    ak3 = ak_bf.reshape(bsz, seq, A_KV * A_DH)
    av3 = av_bf.reshape(bsz, seq, A_KV * A_DH)
    rowblk = lambda b, q: (b * nb + q, 0)
    rows = A_GROUP * Q_BLOCK
    return pl.pallas_call(
        functools.partial(_dsa_prompt_kernel, topk=topk),
        grid=(bsz, nb),
        in_specs=[
            pl.BlockSpec((Q_BLOCK, I_HEADS * I_DIM), rowblk),
            pl.BlockSpec((Q_BLOCK, LANE), rowblk),
            pl.BlockSpec((1, seq, I_DIM), lambda b, q: (b, 0, 0)),
            pl.BlockSpec((Q_BLOCK, A_HEADS * A_DH), rowblk),
            pl.BlockSpec((1, seq, A_KV * A_DH), lambda b, q: (b, 0, 0)),
            pl.BlockSpec((1, seq, A_KV * A_DH), lambda b, q: (b, 0, 0)),
        ],
        out_specs=pl.BlockSpec((Q_BLOCK, A_HEADS * A_DH), rowblk),
        out_shape=jax.ShapeDtypeStruct((bsz * seq, A_HEADS * A_DH), BF16),
        scratch_shapes=[
            pltpu.VMEM((seq, Q_BLOCK), I32),
            pltpu.VMEM((Q_BLOCK, seq), F32),
            pltpu.VMEM((rows, LANE), F32),
            pltpu.VMEM((rows, 2 * A_DH), F32),
        ],
        compiler_params=_cparams(("parallel", "arbitrary")),
        name="dsa_prompt",
    )(iq, iw, ik3, aq, ak3, av3)


IDX_PAGES = 8
ATT_PAGES = 4
PAGE_ROWS = PAGE * A_KV


def _dsa_sample_index_kernel(pt_ref, *refs, n_pages, topk):
    page_refs = refs[:IDX_PAGES]
    iqt_ref, w_ref, iknew_ref, sel_ref, selnew_ref, sc_ref = refs[IDX_PAGES:]
    g = pl.program_id(1)

    @pl.when(g == 0)
    def _():
        sc_ref[...] = jnp.zeros_like(sc_ref)

    iqt = iqt_ref[0]
    w = w_ref[0]
    lane = lax.broadcasted_iota(I32, (PAGE, LANE), 1)

    def col_score(keys_f32):
        s = jnp.dot(keys_f32.astype(BF16), iqt, preferred_element_type=F32)
        return jnp.sum(jnp.maximum(s, 0.0) * w, axis=1, keepdims=True)

    acc = sc_ref[...]
    for j in range(IDX_PAGES):
        col = col_score(page_refs[j][0])
        acc = acc + jnp.where(lane == g * IDX_PAGES + j, col, 0.0)
    sc_ref[...] = acc

    @pl.when(g == n_pages // IDX_PAGES - 1)
    def _():
        past = _order_key(sc_ref[...])
        row = lax.broadcasted_iota(I32, (PAGE, 1), 0)
        new = jnp.where(row == 0, col_score(iknew_ref[0]), -jnp.inf)
        newk = _order_key(new)

        def count_ge(cand):
            hits = jnp.sum(jnp.where(past >= cand, 1, 0).astype(I32), axis=0, keepdims=True)
            hits = jnp.sum(hits, axis=1, keepdims=True)
            return hits + jnp.sum(jnp.where(newk >= cand, 1, 0).astype(I32), axis=0, keepdims=True)

        thr = _kth_largest_key(count_ge, topk, (1, 1))
        thr = jnp.maximum(thr, NEGINF_KEY + 1)
        sel = jnp.where(past >= thr, 1.0, 0.0).astype(BF16)
        r = lax.broadcasted_iota(I32, (PAGE_ROWS, PAGE), 0)
        k = lax.broadcasted_iota(I32, (PAGE_ROWS, PAGE), 1)
        expand = jnp.where(r // A_KV == k, 1.0, 0.0).astype(BF16)
        sel_ref[0] = jnp.dot(expand, sel, preferred_element_type=F32).astype(sel_ref.dtype)
        new_sel = jnp.where(newk[0:1, :] >= thr, 1.0, 0.0)
        rr = lax.broadcasted_iota(I32, (PAGE_ROWS, LANE), 0)
        selnew_ref[0] = jnp.where(rr < A_KV, new_sel, 0.0)


def _dsa_sample_attn_kernel(pt_ref, *refs, n_pages):
    k_refs = refs[:ATT_PAGES]
    v_refs = refs[ATT_PAGES:2 * ATT_PAGES]
    qt_ref, sel_ref, selnew_ref, knew_ref, vnew_ref, o_ref, m_ref, l_ref, acc_ref = refs[2 * ATT_PAGES:]
    g = pl.program_id(1)

    @pl.when(g == 0)
    def _():
        m_ref[...] = jnp.full(m_ref.shape, NEG_BIG, F32)
        l_ref[...] = jnp.zeros_like(l_ref)
        acc_ref[...] = jnp.zeros_like(acc_ref)

    qt = qt_ref[0]
    r = lax.broadcasted_iota(I32, (PAGE_ROWS, LANE), 0)
    lane = lax.broadcasted_iota(I32, (PAGE_ROWS, LANE), 1)
    head_bias = jnp.where(r % A_KV == lane // A_GROUP, 0.0, NEG_BIG)
    pick_r = lax.broadcasted_iota(I32, (PAGE, LANE), 0)
    tn = (((0,), (0,)), ((), ()))

    def step(k_rows, v_rows, sel_b):
        s = jnp.dot(k_rows.astype(BF16), qt, preferred_element_type=F32) + head_bias + (sel_b - 1.0) * (-NEG_BIG)
        m_old = m_ref[...]
        m_new = jnp.maximum(m_old, jnp.max(s, axis=0, keepdims=True))
        p = jnp.exp2(s - m_new)
        alpha = jnp.exp2(m_old - m_new)
        l_ref[...] = alpha * l_ref[...] + jnp.sum(p, axis=0, keepdims=True)
        pv = lax.dot_general(v_rows.astype(BF16), p.astype(BF16), tn, preferred_element_type=F32)
        acc_ref[...] = alpha * acc_ref[...] + pv
        m_ref[...] = m_new

    sel = sel_ref[0]
    for j in range(ATT_PAGES):
        onehot = jnp.where(pick_r == g * ATT_PAGES + j, 1.0, 0.0).astype(BF16)
        sel_b = jnp.dot(sel, onehot, preferred_element_type=F32)
        step(k_refs[j][0], v_refs[j][0], sel_b)

    @pl.when(g == n_pages // ATT_PAGES - 1)
    def _():
        step(knew_ref[0], vnew_ref[0], selnew_ref[0])
        o_ref[0] = acc_ref[...] / l_ref[...]


def _dsa_sample(aq, ak, av, iq, iw, ik, pool_k, pool_v, pool_kidx, page_table):
    bd, n_pages = page_table.shape
    n_pool = pool_k.shape[0]
    topk = min(TOPK_MAX, (n_pages * PAGE + 1) // 4)
    assert n_pages == LANE and n_pages % IDX_PAGES == 0 and n_pages % ATT_PAGES == 0
    iqt = jnp.zeros((bd, I_DIM, LANE), BF16).at[:, :, :I_HEADS].set(
        iq.reshape(bd, I_HEADS, I_DIM).transpose(0, 2, 1))
    w_row = jnp.where(jnp.arange(LANE) < I_HEADS, iw, 0.0).reshape(bd, 1, LANE)
    ik_new = jnp.zeros((bd, PAGE, I_DIM), F32).at[:, 0, :].set(ik)
    new_rows = lambda a: jnp.zeros((bd, PAGE_ROWS, A_DH), F32).at[:, :A_KV, :].set(a.reshape(bd, A_KV, A_DH))
    k_new, v_new = new_rows(ak), new_rows(av)
    qt = jnp.zeros((bd, A_DH, LANE), BF16).at[:, :, :A_HEADS].set(
        aq.reshape(bd, A_HEADS, A_DH).transpose(0, 2, 1))

    def page_spec(rows, width, j, per):
        return pl.BlockSpec((1, rows, width), lambda b, g, pt: (pt[b, g * per + j], 0, 0))

    per_row = lambda shape: pl.BlockSpec((1,) + shape, lambda b, g, pt: (b, 0, 0))
    sel, sel_new = pl.pallas_call(
        functools.partial(_dsa_sample_index_kernel, n_pages=n_pages, topk=topk),
        grid_spec=pltpu.PrefetchScalarGridSpec(
            num_scalar_prefetch=1,
            grid=(bd, n_pages // IDX_PAGES),
            in_specs=[page_spec(PAGE, I_DIM, j, IDX_PAGES) for j in range(IDX_PAGES)]
            + [per_row((I_DIM, LANE)), per_row((1, LANE)), per_row((PAGE, I_DIM))],
            out_specs=[per_row((PAGE_ROWS, PAGE)), per_row((PAGE_ROWS, LANE))],
            scratch_shapes=[pltpu.VMEM((PAGE, LANE), F32)],
        ),
        out_shape=[jax.ShapeDtypeStruct((bd, PAGE_ROWS, PAGE), BF16),
                   jax.ShapeDtypeStruct((bd, PAGE_ROWS, LANE), F32)],
        compiler_params=_cparams(("parallel", "arbitrary")),
        name="dsa_sample_index",
    )(page_table, *([pool_kidx] * IDX_PAGES), iqt, w_row, ik_new)

    pk = pool_k.reshape(n_pool, PAGE_ROWS, A_DH)
    pv = pool_v.reshape(n_pool, PAGE_ROWS, A_DH)
    o_t = pl.pallas_call(
        functools.partial(_dsa_sample_attn_kernel, n_pages=n_pages),
        grid_spec=pltpu.PrefetchScalarGridSpec(
            num_scalar_prefetch=1,
            grid=(bd, n_pages // ATT_PAGES),
            in_specs=[page_spec(PAGE_ROWS, A_DH, j, ATT_PAGES) for j in range(ATT_PAGES)] * 2
            + [per_row((A_DH, LANE)), per_row((PAGE_ROWS, PAGE)), per_row((PAGE_ROWS, LANE)),
               per_row((PAGE_ROWS, A_DH)), per_row((PAGE_ROWS, A_DH))],
            out_specs=per_row((A_DH, LANE)),
            scratch_shapes=[pltpu.VMEM((1, LANE), F32), pltpu.VMEM((1, LANE), F32), pltpu.VMEM((A_DH, LANE), F32)],
        ),
        out_shape=jax.ShapeDtypeStruct((bd, A_DH, LANE), F32),
        compiler_params=_cparams(("parallel", "arbitrary")),
        name="dsa_sample_attn",
    )(page_table, *([pk] * ATT_PAGES), *([pv] * ATT_PAGES), qt, sel, sel_new, k_new, v_new)
    return o_t[:, :, :A_HEADS].transpose(0, 2, 1).reshape(bd, A_HEADS * A_DH).astype(BF16)


def _gate_mix_kernel(rn_ref, att_ref, wr_ref, wa_ref, gr_ref, ga_ref, o_ref):
    r = jnp.dot(rn_ref[...], wr_ref[...], preferred_element_type=F32)
    a = jnp.dot(att_ref[...], wa_ref[...], preferred_element_type=F32)
    o_ref[...] = (gr_ref[...].astype(F32) * r + ga_ref[...].astype(F32) * a).astype(o_ref.dtype)


def _out_proj_kernel(x_ref, mix_ref, wo_ref, nw_ref, h_ref, hn_ref):
    h = x_ref[...] + jnp.dot(mix_ref[...], wo_ref[...], preferred_element_type=F32)
    h_ref[...] = h
    ms = jnp.mean(h * h, axis=-1, keepdims=True)
    hn_ref[...] = (h * lax.rsqrt(ms + EPS) * nw_ref[...]).astype(hn_ref.dtype)


def _mixer_out(x2d, rn, att, gates, wr, wa, wo, norm2_w, tm):
    n, d = x2d.shape
    tn = 512
    ncol = d // tn
    mixed = pl.pallas_call(
        _gate_mix_kernel,
        grid=(ncol, n // tm),
        in_specs=[
            pl.BlockSpec((tm, rn.shape[1]), lambda j, i: (i, 0)),
            pl.BlockSpec((tm, att.shape[1]), lambda j, i: (i, 0)),
            pl.BlockSpec((wr.shape[0], tn), lambda j, i: (0, j)),
            pl.BlockSpec((wa.shape[0], tn), lambda j, i: (0, j)),
            pl.BlockSpec((tm, tn), lambda j, i: (i, j)),
            pl.BlockSpec((tm, tn), lambda j, i: (i, ncol + j)),
        ],
        out_specs=pl.BlockSpec((tm, tn), lambda j, i: (i, j)),
        out_shape=jax.ShapeDtypeStruct((n, d), BF16),
        compiler_params=_cparams(("parallel", "parallel")),
        name="gate_mix",
    )(rn, att, wr, wa, gates, gates)
    tm2 = min(tm, 256)
    h, hn = pl.pallas_call(
        _out_proj_kernel,
        grid=(n // tm2,),
        in_specs=[
            pl.BlockSpec((tm2, d), lambda i: (i, 0)),
            pl.BlockSpec((tm2, d), lambda i: (i, 0)),
            pl.BlockSpec((d, d), lambda i: (0, 0)),
            pl.BlockSpec((1, d), lambda i: (0, 0)),
        ],
        out_specs=[pl.BlockSpec((tm2, d), lambda i: (i, 0)), pl.BlockSpec((tm2, d), lambda i: (i, 0))],
        out_shape=[jax.ShapeDtypeStruct((n, d), F32), jax.ShapeDtypeStruct((n, d), BF16)],
        compiler_params=_cparams(("parallel",)),
        name="out_proj",
    )(x2d, mixed, wo, norm2_w.reshape(1, d))
    return h, hn


NO_RANK = 127.0


def _top_desc(x, count):
    vals = []
    cur = x
    for _ in range(count):
        mx = jnp.max(cur, axis=0, keepdims=True)
        vals.append(mx)
        cur = jnp.where(cur == mx, -jnp.inf, cur)
    return vals


def _top_desc_rank(x, count):
    vals = []
    cur = x
    rank = jnp.full(x.shape, NO_RANK, F32)
    for i in range(count):
        mx = jnp.max(cur, axis=0, keepdims=True)
        hit = cur == mx
        vals.append(mx)
        rank = jnp.where(hit, float(i), rank)
        cur = jnp.where(hit, -jnp.inf, cur)
    return vals, rank


def _peer_route_kernel(hn_ref, wq_ref, k1_ref, k2_ref, n1_ref, e1_ref, r2_ref, e2_ref):
    q = jnp.dot(hn_ref[...], wq_ref[...], preferred_element_type=F32).astype(BF16)
    nt = (((1,), (1,)), ((), ()))
    half = P_TOPK // 2
    for h in range(P_HEADS):
        qh = q[:, h * P_DKEY:(h + 1) * P_DKEY]
        s1 = lax.dot_general(k1_ref[h], qh, nt, preferred_element_type=F32)
        s2 = lax.dot_general(k2_ref[h], qh, nt, preferred_element_type=F32)
        v1, r1 = _top_desc_rank(s1, P_TOPK)
        v2, r2 = _top_desc_rank(s2, P_TOPK)
        v1blk = jnp.concatenate(v1, axis=0)
        v2blk = jnp.concatenate(v2, axis=0)
        blocks = ([v1[0] + v2blk] + [v1[i] + v2blk[:half] for i in range(1, half)]
                  + [v1blk[half:] + v2[0]])
        c = _top_desc(jnp.concatenate(blocks, axis=0), P_TOPK)
        thr = c[-1]
        z = jnp.zeros_like(thr)
        for i in range(P_TOPK):
            z = z + jnp.exp(c[i] - c[0])
        counts = [jnp.sum(jnp.where(blk >= thr, 1.0, 0.0), axis=0, keepdims=True) for blk in blocks[:half]]
        tail = jnp.where(blocks[half] >= thr, 1.0, 0.0)
        counts += [tail[i:i + 1] for i in range(half)]
        n1 = jnp.zeros_like(s1)
        for i in range(P_TOPK):
            n1 = jnp.where(r1 == float(i), counts[i], n1)
        n1_ref[h] = n1
        e1_ref[h] = jnp.where(r1 < P_TOPK, jnp.exp(s1 - v1[0]) / z, 0.0)
        r2_ref[h] = r2.astype(r2_ref.dtype)
        e2_ref[h] = jnp.where(r2 < P_TOPK, jnp.exp(s2 - v2[0]), 0.0).astype(e2_ref.dtype)


def _peer_expert_kernel(hn_ref, h_ref, u_ref, vt_ref, n1_ref, e1_ref, r2_ref, e2_ref,
                        y_ref, acc_ref, act_ref, p_ref, *, a_per_tile):
    j = pl.program_id(1)

    @pl.when(j == 0)
    def _():
        acc_ref[...] = jnp.zeros_like(acc_ref)

    nt = (((1,), (1,)), ((), ()))
    x = lax.dot_general(u_ref[...], hn_ref[...], nt, preferred_element_type=F32)
    act_ref[...] = (0.5 * x * (1.0 + lax.erf(x * (2.0 ** -0.5)))).astype(act_ref.dtype)
    tm = act_ref.shape[1]

    for al in range(a_per_tile):
        a = j * a_per_tile + al
        w = jnp.zeros((P_NKEYS, tm), BF16)
        for h in range(P_HEADS):
            nrow = jnp.broadcast_to(n1_ref[h, pl.ds(a, 1), :].astype(BF16), (P_NKEYS, tm))
            erow = jnp.broadcast_to(e1_ref[h, pl.ds(a, 1), :].astype(BF16), (P_NKEYS, tm))
            w = w + jnp.where(r2_ref[h] < nrow, e2_ref[h], jnp.zeros((), BF16)) * erow
        p_ref[al * P_NKEYS:(al + 1) * P_NKEYS, :] = w * act_ref[al * P_NKEYS:(al + 1) * P_NKEYS, :]
    acc_ref[...] +=jnp.dot(vt_ref[...], p_ref[...], preferred_element_type=F32)

    @pl.when(j == pl.num_programs(1) - 1)
    def _():
        y_ref[...] = h_ref[...] + acc_ref[...].T


def _peer(hn, h, wq, k1p, k2p, u_bf, vt_bf, tm, te):
    n, d = hn.shape
    hk = P_HEADS * P_DKEY
    tr = min(tm, 128)
    tok = lambda i: (0, 0, i)
    head_blk = pl.BlockSpec((P_HEADS, P_NKEYS, tr), tok)
    n1, e1, r2, e2 = pl.pallas_call(
        _peer_route_kernel,
        grid=(n // tr,),
        in_specs=[
            pl.BlockSpec((tr, d), lambda i: (i, 0)),
            pl.BlockSpec((d, hk), lambda i: (0, 0)),
            pl.BlockSpec((P_HEADS, P_NKEYS, P_DKEY), lambda i: (0, 0, 0)),
            pl.BlockSpec((P_HEADS, P_NKEYS, P_DKEY), lambda i: (0, 0, 0)),
        ],
        out_specs=[head_blk] * 4,
        out_shape=[jax.ShapeDtypeStruct((P_HEADS, P_NKEYS, n), dt) for dt in (F32, F32, BF16, BF16)],
        compiler_params=_cparams(("parallel",)),
        name="peer_route",
    )(hn, wq, k1p, k2p)

    n_exp = u_bf.shape[0]
    once = pl.Buffered(1)
    tok2 = lambda i, j: (0, 0, i)
    head_blk2 = pl.BlockSpec((P_HEADS, P_NKEYS, tm), tok2, pipeline_mode=once)
    return pl.pallas_call(
        functools.partial(_peer_expert_kernel, a_per_tile=te // P_NKEYS),
        grid=(n // tm, n_exp // te),
        in_specs=[
            pl.BlockSpec((tm, d), lambda i, j: (i, 0), pipeline_mode=once),
            pl.BlockSpec((tm, d), lambda i, j: (i, 0), pipeline_mode=once),
            pl.BlockSpec((te, d), lambda i, j: (j, 0)),
            pl.BlockSpec((d, te), lambda i, j: (0, j)),
            head_blk2, head_blk2, head_blk2, head_blk2,
        ],
        out_specs=pl.BlockSpec((tm, d), lambda i, j: (i, 0)),
        out_shape=jax.ShapeDtypeStruct((n, d), F32),
        scratch_shapes=[pltpu.VMEM((d, tm), F32), pltpu.VMEM((te, tm), BF16), pltpu.VMEM((te, tm), BF16)],
        compiler_params=_cparams(("parallel", "arbitrary")),
        name="peer_experts",
    )(hn, h, u_bf, vt_bf, n1, e1, r2, e2)


def _prep_weights(w_in, w_ret_o, w_att_o, w_out, peer_wq, peer_keys1, peer_keys2, peer_u, peer_v):
    iw_pad = jnp.pad(w_in[:, IW_COL:GATE_COL], ((0, 0), (0, 3 * LANE - I_HEADS)))
    w_bf = jnp.concatenate([w_in[:, :IW_COL], iw_pad, w_in[:, GATE_COL:]], axis=1).astype(BF16)
    half = P_DKEY // 2
    k1p = jnp.pad(peer_keys1, ((0, 0), (0, 0), (0, half))).astype(BF16)
    k2p = jnp.pad(peer_keys2, ((0, 0), (0, 0), (half, 0))).astype(BF16)
    return dict(w_bf=w_bf, wr=w_ret_o.astype(BF16), wa=w_att_o.astype(BF16), wo=w_out.astype(BF16),
                wq=peer_wq.astype(BF16), k1p=k1p, k2p=k2p,
                u_bf=peer_u.astype(BF16), vt_bf=peer_v.T.astype(BF16))


def _prompt_layer(x, W, norm1_w, q_norm_w, k_norm_w, ret_gn_w, norm2_w):
    bsz, seq, d = x.shape
    x2d = x.reshape(bsz * seq, d)
    tm = 512
    p = _mixer_in(x2d, jnp.arange(seq), norm1_w, W["w_bf"], q_norm_w, k_norm_w, tm)
    s0 = jnp.zeros((bsz, R_HEADS, R_DK, R_DV), F32)
    rn, S = _retention_prompt(p["rq"], p["rk"], p["rvg"], ret_gn_w, s0, bsz, seq)
    att = _dsa_prompt(p["iq"], p["iw"], p["ik_bf"], p["aq"], p["ak_bf"], p["av_bf"], bsz, seq)
    h, hn = _mixer_out(x2d, rn, att, p["gates"], W["wr"], W["wa"], W["wo"], norm2_w, tm)
    y = _peer(hn, h, W["wq"], W["k1p"], W["k2p"], W["u_bf"], W["vt_bf"], tm=512, te=1024)
    return (y.reshape(bsz, seq, d), p["ak"].reshape(bsz, seq, A_KV, A_DH), p["av"].reshape(bsz, seq, A_KV, A_DH),
            p["ik"].reshape(bsz, seq, I_DIM), S)


def _sample_layer(x, pool_k, pool_v, pool_kidx, s_in, page_table, W, norm1_w, q_norm_w, k_norm_w,
                  ret_gn_w, norm2_w):
    bd, ts, d = x.shape
    assert ts == 1
    x2d = x.reshape(bd, d)
    past = page_table.shape[1] * PAGE
    p = _mixer_in(x2d, jnp.full((bd,), past, I32), norm1_w, W["w_bf"], q_norm_w, k_norm_w, bd)
    rn, S = _retention_sample(p["rq"], p["rk"], p["rvg"], ret_gn_w, s_in)
    att = _dsa_sample(p["aq"], p["ak"], p["av"], p["iq"], p["iw"], p["ik"], pool_k, pool_v, pool_kidx, page_table)
    h, hn = _mixer_out(x2d, rn, att, p["gates"], W["wr"], W["wa"], W["wo"], norm2_w, bd)
    pad = LANE - bd
    hn_p = jnp.pad(hn, ((0, pad), (0, 0)))
    h_p = jnp.pad(h, ((0, pad), (0, 0)))
    y = _peer(hn_p, h_p, W["wq"], W["k1p"], W["k2p"], W["u_bf"], W["vt_bf"], tm=LANE, te=1024)[:bd]
    return (y.reshape(bd, ts, d), p["ak"].reshape(bd, ts, A_KV, A_DH), p["av"].reshape(bd, ts, A_KV, A_DH),
            p["ik"].reshape(bd, ts, I_DIM), S)


def kernel(x_prompt, x_sample, cache_k, cache_v, cache_k_idx, state_ret, page_table, norm1_w, w_in, q_norm_w,
           k_norm_w, ret_gn_w, w_ret_o, w_att_o, w_out, norm2_w, peer_wq, peer_keys1, peer_keys2, peer_u, peer_v):
    depth = w_in.shape[0]
    assert depth == 1
    l = 0
    W = _prep_weights(w_in[l], w_ret_o[l], w_att_o[l], w_out[l], peer_wq[l], peer_keys1[l], peer_keys2[l],
                      peer_u[l], peer_v[l])
    yp, kp, vp, ikp, sp = _prompt_layer(x_prompt, W, norm1_w[l], q_norm_w[l], k_norm_w[l], ret_gn_w[l], norm2_w[l])
    ys, kn, vn, ikn, sn = _sample_layer(x_sample, cache_k[l], cache_v[l], cache_k_idx[l], state_ret[l], page_table,
                                        W, norm1_w[l], q_norm_w[l], k_norm_w[l], ret_gn_w[l], norm2_w[l])
    return (yp, ys, kp[None], vp[None], ikp[None], sp[None], kn[None], vn[None], ikn[None], sn[None])
```

```python
import functools

import jax
import jax.numpy as jnp
from jax import lax
from jax.experimental import pallas as pl
from jax.experimental.pallas import tpu as pltpu

F32 = jnp.float32
BF16 = jnp.bfloat16
I32 = jnp.int32

D_MODEL = 2048
PAGE = 128
R_HEADS, R_DK, R_DV, R_CHUNK = 8, 128, 256, 128
A_HEADS, A_KV, A_DH = 16, 4, 128
A_GROUP = A_HEADS // A_KV
I_HEADS, I_DIM = 16, 128
TOPK_MAX = 256
Q_BLOCK = 128
ROPE_THETA = 10000.0
P_HEADS, P_NKEYS, P_DKEY, P_TOPK = 8, 128, 128, 16
EPS = 1e-6

LANE = 128
VMEM_LIMIT = 56 * 1024 * 1024
NEG_BIG = -1e30
LOG2E = 1.4426950408889634
INT_MIN = -2147483648
NEGINF_KEY = INT_MIN + 0x7FFFFF

SEG = dict(rq=(0, 1024), rk=(1024, 1024), rvg=(2048, 4096), aq=(6144, 2048), ak=(8192, 512), av=(8704, 512),
           iq=(9216, 2048), ik=(11264, 128), iw=(11392, 128))
GATE_COL = 11408
GATE_WIDTH = 2 * D_MODEL


def _cparams(sem):
    return pltpu.CompilerParams(dimension_semantics=sem, vmem_limit_bytes=VMEM_LIMIT)


def _rmsnorm_kernel(x_ref, w_ref, o_ref):
    x = x_ref[...]
    ms = jnp.mean(x * x, axis=-1, keepdims=True)
    o_ref[...] = (x * lax.rsqrt(ms + EPS) * w_ref[...]).astype(o_ref.dtype)


def _rmsnorm(x, w, tm):
    n, d = x.shape
    return pl.pallas_call(
        _rmsnorm_kernel,
        grid=(n // tm,),
        in_specs=[pl.BlockSpec((tm, d), lambda i: (i, 0)), pl.BlockSpec((1, d), lambda i: (0, 0))],
        out_specs=pl.BlockSpec((tm, d), lambda i: (i, 0)),
        out_shape=jax.ShapeDtypeStruct((n, d), BF16),
        compiler_params=_cparams(("parallel",)),
        name="rmsnorm",
    )(x, w.reshape(1, d))


def _proj_kernel(x_ref, w_ref, a_ref, b_ref, *refs, mode, scale, heads):
    out_refs, wb_ref = refs[:-1], refs[-1]

    @pl.when(pl.program_id(1) == 0)
    def _():
        wb_ref[...] = w_ref[...].astype(wb_ref.dtype)

    z = jnp.dot(x_ref[...], wb_ref[...], preferred_element_type=F32)
    if mode in ("rope", "norm_rope"):
        a, b = a_ref[...], b_ref[...]
        parts = []
        for h in range(heads):
            zh = z[:, h * LANE:(h + 1) * LANE]
            y = zh * a + pltpu.roll(zh, LANE // 2, 1) * b
            if mode == "norm_rope":
                sq = zh * zh
                hi = sq.astype(BF16)
                lo = (sq - hi.astype(F32)).astype(BF16)
                avg = jnp.full((LANE, LANE), 1.0 / LANE, BF16)
                ms = (jnp.dot(hi, avg, preferred_element_type=F32) + jnp.dot(lo, avg, preferred_element_type=F32))
                y = y * lax.rsqrt(ms + EPS)
            parts.append(y)
        z = jnp.concatenate(parts, axis=1) if heads > 1 else parts[0]
    elif mode == "sigmoid":
        z = jax.nn.sigmoid(z)
    elif scale != 1.0:
        z = z * scale
    for o in out_refs:
        o[...] = z.astype(o.dtype)


def _proj(xn, w, seg, a_tab, b_tab, *, mode, scale=1.0, out_dtypes, tm, tn):
    n, d = xn.shape
    col0, width = seg
    tn = min(tn, width)
    assert col0 % tn == 0 and width % tn == 0 and n % tm == 0
    c0t = col0 // tn
    pos_blocks = a_tab.shape[0] // tm
    kern = functools.partial(_proj_kernel, mode=mode, scale=scale, heads=tn // LANE)
    outs = pl.pallas_call(
        kern,
        grid=(width // tn, n // tm),
        in_specs=[
            pl.BlockSpec((tm, d), lambda j, i: (i, 0)),
            pl.BlockSpec((d, tn), lambda j, i: (0, c0t + j)),
            pl.BlockSpec((tm, LANE), lambda j, i: (i % pos_blocks, 0)),
            pl.BlockSpec((tm, LANE), lambda j, i: (i % pos_blocks, 0)),
        ],
        out_specs=[pl.BlockSpec((tm, tn), lambda j, i: (i, j)) for _ in out_dtypes],
        out_shape=[jax.ShapeDtypeStruct((n, width), dt) for dt in out_dtypes],
        scratch_shapes=[pltpu.VMEM((d, tn), BF16)],
        compiler_params=_cparams(("parallel", "arbitrary")),
        name="proj_" + mode,
    )(xn, w, a_tab, b_tab)
    return outs


def _rope_tables(pos):
    half = LANE // 2
    freqs = ROPE_THETA ** (-jnp.arange(half, dtype=F32) / half)
    ang = pos.astype(F32)[:, None] * freqs[None, :]
    cos, sin = jnp.cos(ang), jnp.sin(ang)
    return jnp.concatenate([cos, cos], axis=1), jnp.concatenate([-sin, sin], axis=1)


def _mixer_in(x2d, pos_rows, norm1_w, w_in, w_gate, q_norm_w, k_norm_w, tm):
    xn = _rmsnorm(x2d, norm1_w, tm)
    cos2, sin2 = _rope_tables(pos_rows)
    swap = lambda g: jnp.roll(g, LANE // 2)
    rope = lambda gain, scale: (cos2 * (gain * scale)[None, :], sin2 * (swap(gain) * scale)[None, :])
    ones = jnp.ones((LANE,), F32)
    p = functools.partial(_proj, xn, tm=tm, tn=512)
    out = {}
    out["rq"], = p(w_in, SEG["rq"], *rope(ones, 1.0), mode="rope", out_dtypes=(BF16,))
    out["rk"], = p(w_in, SEG["rk"], *rope(ones, R_DK ** -0.5), mode="rope", out_dtypes=(BF16,))
    out["rvg"], = p(w_in, SEG["rvg"], cos2, sin2, mode="plain", out_dtypes=(BF16,))
    out["aq"], = p(w_in, SEG["aq"], *rope(q_norm_w, A_DH ** -0.5 * LOG2E), mode="norm_rope", out_dtypes=(BF16,))
    out["ak"], out["ak_bf"] = p(w_in, SEG["ak"], *rope(k_norm_w, 1.0), mode="norm_rope", out_dtypes=(F32, BF16))
    out["av"], out["av_bf"] = p(w_in, SEG["av"], cos2, sin2, mode="plain", out_dtypes=(F32, BF16))
    out["iq"], = p(w_in, SEG["iq"], *rope(ones, I_DIM ** -0.5), mode="rope", out_dtypes=(BF16,))
    out["ik"], out["ik_bf"] = p(w_in, SEG["ik"], *rope(ones, 1.0), mode="rope", out_dtypes=(F32, BF16))
    out["iw"], = p(w_in, SEG["iw"], cos2, sin2, mode="plain", scale=I_HEADS ** -0.5, out_dtypes=(F32,))
    out["gates"], = p(w_gate, (0, GATE_WIDTH), cos2, sin2, mode="sigmoid", out_dtypes=(BF16,))
    return out


def _ret_tables(chunk):
    log_g = jnp.log1p(-jnp.exp2(-5.0 - jnp.arange(R_HEADS, dtype=F32)))
    n = jnp.arange(chunk, dtype=F32)
    diff = n[:, None] - n[None, :]
    causal = diff >= 0
    dmat = jnp.where(causal, jnp.exp(jnp.where(causal, diff, 0.0)[None] * log_g[:, None, None]), 0.0)
    qdec = jnp.exp((n + 1.0)[None, :] * log_g[:, None])
    kdec = jnp.exp((chunk - 1.0 - n)[None, :] * log_g[:, None])
    gc = jnp.exp(chunk * log_g)
    return dmat, qdec, kdec, gc


def _ret_kernel(q_ref, k_ref, v_ref, g_ref, gnw_ref, dmat_ref, qdec_ref, kdec_ref, gc_ref, s0_ref,
                o_ref, s_ref):
    c = pl.program_id(1)

    @pl.when(c == 0)
    def _():
        s_ref[...] = s0_ref[...]

    nt = (((1,), (1,)), ((), ()))
    tn = (((0,), (0,)), ((), ()))
    for h in range(R_HEADS):
        S = s_ref[0, h]
        q = q_ref[:, h * R_DK:(h + 1) * R_DK]
        k = k_ref[:, h * R_DK:(h + 1) * R_DK]
        v = v_ref[:, h * R_DV:(h + 1) * R_DV]
        scores = lax.dot_general(q, k, nt, preferred_element_type=F32) * dmat_ref[h]
        intra = jnp.dot(scores.astype(BF16), v, preferred_element_type=F32)
        qd = (q.astype(F32) * qdec_ref[h]).astype(BF16)
        cross = jnp.dot(qd, S.astype(BF16), preferred_element_type=F32)
        kd = (k.astype(F32) * kdec_ref[h]).astype(BF16)
        s_ref[0, h] = gc_ref[h] * S + lax.dot_general(kd, v, tn, preferred_element_type=F32)
        o = intra + cross
        mu = jnp.mean(o, axis=-1, keepdims=True)
        var = jnp.mean(jnp.square(o - mu), axis=-1, keepdims=True)
        g = g_ref[:, h * R_DV:(h + 1) * R_DV].astype(F32)
        rn = (o - mu) * lax.rsqrt(var + EPS) * gnw_ref[:, h * R_DV:(h + 1) * R_DV] * (g * jax.nn.sigmoid(g))
        o_ref[:, h * R_DV:(h + 1) * R_DV] = rn.astype(o_ref.dtype)


def _retention_prompt(rq, rk, rvg, gn_w, s0, bsz, seq):
    C = R_CHUNK
    nc = seq // C
    dmat, qdec, kdec, gc = _ret_tables(C)
    qdec = jnp.broadcast_to(qdec[:, :, None], (R_HEADS, C, R_DK))
    kdec = jnp.broadcast_to(kdec[:, :, None], (R_HEADS, C, R_DK))
    gc = jnp.broadcast_to(gc[:, None, None], (R_HEADS, 1, R_DV))
    hk, hv = R_HEADS * R_DK, R_HEADS * R_DV
    row = lambda b, c: (b * nc + c, 0)
    const3 = lambda b, c: (0, 0, 0)
    state = pl.BlockSpec((1, R_HEADS, R_DK, R_DV), lambda b, c: (b, 0, 0, 0))
    rn, S = pl.pallas_call(
        _ret_kernel,
        grid=(bsz, nc),
        in_specs=[
            pl.BlockSpec((C, hk), row),
            pl.BlockSpec((C, hk), row),
            pl.BlockSpec((C, hv), row),
            pl.BlockSpec((C, hv), lambda b, c: (b * nc + c, 1)),
            pl.BlockSpec((1, hv), lambda b, c: (0, 0)),
            pl.BlockSpec((R_HEADS, C, C), const3),
            pl.BlockSpec((R_HEADS, C, R_DK), const3),
            pl.BlockSpec((R_HEADS, C, R_DK), const3),
            pl.BlockSpec((R_HEADS, 1, R_DV), const3),
            state,
        ],
        out_specs=[pl.BlockSpec((C, hv), row), state],
        out_shape=[jax.ShapeDtypeStruct((bsz * seq, hv), BF16),
                   jax.ShapeDtypeStruct((bsz, R_HEADS, R_DK, R_DV), F32)],
        compiler_params=_cparams(("parallel", "arbitrary")),
        name="retention",
    )(rq, rk, rvg, rvg, gn_w.reshape(1, -1), dmat, qdec, kdec, gc, s0)
    return rn, S


def _ret_step_kernel(qc_ref, kc_ref, v_ref, g_ref, gnw_ref, gam_ref, s0_ref, o_ref, s_ref):
    for h in range(R_HEADS):
        S = s0_ref[0, h]
        qc, kc = qc_ref[0, h], kc_ref[0, h]
        v = v_ref[0, h]
        gam = gam_ref[h]
        qk = jnp.sum(qc * kc, axis=0, keepdims=True)
        cross = jnp.sum((qc * gam) * S, axis=0, keepdims=True)
        o = qk * v + cross
        s_ref[0, h] = gam * S + kc * v
        mu = jnp.mean(o, axis=-1, keepdims=True)
        var = jnp.mean(jnp.square(o - mu), axis=-1, keepdims=True)
        g = g_ref[0, h]
        o_ref[0, h] = (o - mu) * lax.rsqrt(var + EPS) * gnw_ref[0, h] * (g * jax.nn.sigmoid(g))


def _retention_sample(rq, rk, rvg, gn_w, s0):
    bd = rq.shape[0]
    _, qdec, _, _ = _ret_tables(1)
    gam = jnp.broadcast_to(qdec[:, :, None], (R_HEADS, 1, R_DV)).astype(F32)
    qc = rq.astype(F32).reshape(bd, R_HEADS, R_DK, 1)
    kc = rk.astype(F32).reshape(bd, R_HEADS, R_DK, 1)
    rv = rvg[:, :R_HEADS * R_DV].astype(F32).reshape(bd, R_HEADS, 1, R_DV)
    rg = rvg[:, R_HEADS * R_DV:].astype(F32).reshape(bd, R_HEADS, 1, R_DV)
    gnw = gn_w.reshape(1, R_HEADS, 1, R_DV)
    col = pl.BlockSpec((1, R_HEADS, R_DK, 1), lambda b: (b, 0, 0, 0))
    rowv = pl.BlockSpec((1, R_HEADS, 1, R_DV), lambda b: (b, 0, 0, 0))
    st = pl.BlockSpec((1, R_HEADS, R_DK, R_DV), lambda b: (b, 0, 0, 0))
    rn, S = pl.pallas_call(
        _ret_step_kernel,
        grid=(bd,),
        in_specs=[col, col, rowv, rowv,
                  pl.BlockSpec((1, R_HEADS, 1, R_DV), lambda b: (0, 0, 0, 0)),
                  pl.BlockSpec((R_HEADS, 1, R_DV), lambda b: (0, 0, 0)), st],
        out_specs=[rowv, st],
        out_shape=[jax.ShapeDtypeStruct((bd, R_HEADS, 1, R_DV), F32),
                   jax.ShapeDtypeStruct((bd, R_HEADS, R_DK, R_DV), F32)],
        compiler_params=_cparams(("parallel",)),
        name="retention_step",
    )(qc, kc, rv, rg, gnw, gam, s0)
    return rn.reshape(bd, R_HEADS * R_DV).astype(BF16), S


def _key_to_float(key):
    return pltpu.bitcast(key ^ ((key >> 31) & 0x7FFFFFFF), F32)


def _kth_largest(count_ge, k, shape):
    cur = jnp.where(count_ge(jnp.zeros(shape, F32)) >= k, 0, INT_MIN).astype(I32)

    def body(i, cur):
        cand = cur | jnp.left_shift(jnp.int32(1), 30 - i)
        return jnp.where(count_ge(_key_to_float(cand)) >= k, cand, cur)

    key = lax.fori_loop(0, 31, body, cur)
    return jnp.where(key <= NEGINF_KEY, jnp.finfo(F32).min, _key_to_float(jnp.maximum(key, NEGINF_KEY + 1)))


KEY_UNIT = 512
TRIP_KEYS = 1024


def _dsa_prompt_kernel(iq_ref, iw_ref, ik_ref, aq_ref, ak_ref, av_ref, o_ref,
                       keys_ref, bias_ref, m_ref, acc_ref, *, topk):
    qb = pl.program_id(1)
    ntrips = (qb * Q_BLOCK + Q_BLOCK + TRIP_KEYS - 1) // TRIP_KEYS
    nt = (((1,), (1,)), ((), ()))

    iq = iq_ref[...]
    pair = lambda hp: jnp.concatenate([iq[:, (2 * hp) * LANE:(2 * hp + 1) * LANE],
                                       iq[:, (2 * hp + 1) * LANE:(2 * hp + 2) * LANE]], axis=0)
    rhs = [pair(hp) for hp in range(I_HEADS // 2)]
    w_t = iw_ref[...].T
    t_idx = qb * Q_BLOCK + lax.broadcasted_iota(I32, (Q_BLOCK, Q_BLOCK), 1)
    row_iota = lax.broadcasted_iota(I32, (Q_BLOCK, Q_BLOCK), 0)

    def score_trip(t, carry):
        for c in range(TRIP_KEYS // Q_BLOCK):
            off = pl.multiple_of(t * TRIP_KEYS + c * Q_BLOCK, Q_BLOCK)
            ikc = ik_ref[0, pl.ds(off, Q_BLOCK), :]
            sc = jnp.zeros((Q_BLOCK, Q_BLOCK), F32)
            for hp in range(I_HEADS // 2):
                prod = lax.dot_general(ikc, rhs[hp], nt, preferred_element_type=F32)
                sc = sc + (jnp.maximum(prod[:, :Q_BLOCK], 0.0) * w_t[2 * hp:2 * hp + 1, :]
                           + jnp.maximum(prod[:, Q_BLOCK:], 0.0) * w_t[2 * hp + 1:2 * hp + 2, :])
            sc = jnp.where(off + row_iota <= t_idx, sc, -jnp.inf)
            keys_ref[pl.ds(off, Q_BLOCK), :] = sc
        return carry

    lax.fori_loop(0, ntrips, score_trip, 0)

    def count_ge(cand):
        def body(t, acc):
            for c in range(TRIP_KEYS // KEY_UNIT):
                off = pl.multiple_of(t * TRIP_KEYS + c * KEY_UNIT, KEY_UNIT)
                blk = keys_ref[pl.ds(off, KEY_UNIT), :]
                hit = jnp.where(blk >= cand, 1, 0).astype(I32)
                acc = acc + jnp.sum(hit.reshape(KEY_UNIT // 8, 8, Q_BLOCK), axis=0)
            return acc
        acc = lax.fori_loop(0, ntrips, body, jnp.zeros((8, Q_BLOCK), I32))
        return jnp.sum(acc, axis=0, keepdims=True)

    thr = _kth_largest(count_ge, topk, (1, Q_BLOCK))

    def bias_trip(t, carry):
        for c in range(TRIP_KEYS // Q_BLOCK):
            off = pl.multiple_of(t * TRIP_KEYS + c * Q_BLOCK, Q_BLOCK)
            sel = keys_ref[pl.ds(off, Q_BLOCK), :] >= thr
            bias_ref[:, pl.ds(off, Q_BLOCK)] = jnp.where(sel, 0.0, NEG_BIG).astype(F32).T
        return carry

    lax.fori_loop(0, ntrips, bias_trip, 0)

    rows = A_GROUP * Q_BLOCK
    ones_v = jnp.ones((KEY_UNIT, A_DH), BF16)
    subs = TRIP_KEYS // KEY_UNIT

    def logits(n, off):
        qg = jnp.concatenate([aq_ref[:, (n * A_GROUP + g) * A_DH:(n * A_GROUP + g + 1) * A_DH]
                              for g in range(A_GROUP)], axis=0)
        kch = ak_ref[0, pl.ds(off, KEY_UNIT), n * A_DH:(n + 1) * A_DH]
        b = bias_ref[:, pl.ds(off, KEY_UNIT)]
        s = lax.dot_general(qg, kch, nt, preferred_element_type=F32)
        return s + jnp.concatenate([b] * A_GROUP, axis=0)

    m_ref[...] = jnp.full((A_KV, rows, LANE), NEG_BIG, F32)
    acc_ref[...] = jnp.zeros((A_KV, rows, 2 * A_DH), F32)

    def attend(t, carry):
        for n in range(A_KV):
            offs = [pl.multiple_of(t * TRIP_KEYS + c * KEY_UNIT, KEY_UNIT) for c in range(subs)]
            s = [logits(n, off) for off in offs]
            m_old = m_ref[n]
            m_new = m_old
            for sc in s:
                m_new = jnp.maximum(m_new, jnp.max(sc, axis=-1, keepdims=True))
            alpha = jnp.exp2(m_old - m_new)
            m = jnp.concatenate([m_new] * (KEY_UNIT // LANE), axis=1)
            pv = jnp.zeros((rows, 2 * A_DH), F32)
            for sc, off in zip(s, offs):
                p = jnp.exp2(sc - m).astype(BF16)
                vch = av_ref[0, pl.ds(off, KEY_UNIT), n * A_DH:(n + 1) * A_DH]
                pv = pv + jnp.dot(p, jnp.concatenate([vch, ones_v], axis=1), preferred_element_type=F32)
            acc_ref[n] = acc_ref[n] * jnp.concatenate([alpha, alpha], axis=1) + pv
            m_ref[n] = m_new
        return carry

    lax.fori_loop(0, ntrips, attend, 0)
    for n in range(A_KV):
        acc = acc_ref[n]
        o = acc[:, :A_DH] / acc[:, A_DH:]
        for g in range(A_GROUP):
            hh = n * A_GROUP + g
            o_ref[:, hh * A_DH:(hh + 1) * A_DH] = o[g * Q_BLOCK:(g + 1) * Q_BLOCK].astype(o_ref.dtype)


def _dsa_prompt(iq, iw, ik_bf, aq, ak_bf, av_bf, bsz, seq):
    nb = seq // Q_BLOCK
    topk = min(TOPK_MAX, seq // 4)
    assert seq % TRIP_KEYS == 0
    ik3 = ik_bf.reshape(bsz, seq, I_DIM)
    ak3 = ak_bf.reshape(bsz, seq, A_KV * A_DH)
    av3 = av_bf.reshape(bsz, seq, A_KV * A_DH)
    rowblk = lambda b, q: (b * nb + q, 0)
    rows = A_GROUP * Q_BLOCK
    return pl.pallas_call(
        functools.partial(_dsa_prompt_kernel, topk=topk),
        grid=(bsz, nb),
        in_specs=[
            pl.BlockSpec((Q_BLOCK, I_HEADS * I_DIM), rowblk),
            pl.BlockSpec((Q_BLOCK, LANE), rowblk),
            pl.BlockSpec((1, seq, I_DIM), lambda b, q: (b, 0, 0)),
            pl.BlockSpec((Q_BLOCK, A_HEADS * A_DH), rowblk),
            pl.BlockSpec((1, seq, A_KV * A_DH), lambda b, q: (b, 0, 0)),
            pl.BlockSpec((1, seq, A_KV * A_DH), lambda b, q: (b, 0, 0)),
        ],
        out_specs=pl.BlockSpec((Q_BLOCK, A_HEADS * A_DH), rowblk),
        out_shape=jax.ShapeDtypeStruct((bsz * seq, A_HEADS * A_DH), BF16),
        scratch_shapes=[
            pltpu.VMEM((seq, Q_BLOCK), F32),
            pltpu.VMEM((Q_BLOCK, seq), F32),
            pltpu.VMEM((A_KV, rows, LANE), F32),
            pltpu.VMEM((A_KV, rows, 2 * A_DH), F32),
        ],
        compiler_params=_cparams(("parallel", "arbitrary")),
        name="dsa_prompt",
    )(iq, iw, ik3, aq, ak3, av3)


IDX_PAGES = 16
ATT_PAGES = 8
PAGE_ROWS = PAGE * A_KV


def _dsa_sample_index_kernel(pt_ref, *refs, n_pages, topk):
    page_refs = refs[:IDX_PAGES]
    iqt_ref, w_ref, iknew_ref, sel_ref, selnew_ref, sc_ref = refs[IDX_PAGES:]
    g = pl.program_id(1)

    @pl.when(g == 0)
    def _():
        sc_ref[...] = jnp.zeros_like(sc_ref)

    iqt = iqt_ref[0]
    w = w_ref[0]
    lane = lax.broadcasted_iota(I32, (PAGE, LANE), 1)

    def col_score(keys_f32):
        s = jnp.dot(keys_f32.astype(BF16), iqt, preferred_element_type=F32)
        return jnp.sum(jnp.maximum(s, 0.0) * w, axis=1, keepdims=True)

    acc = sc_ref[...]
    for j in range(IDX_PAGES):
        col = col_score(page_refs[j][0])
        acc = acc + jnp.where(lane == g * IDX_PAGES + j, col, 0.0)
    sc_ref[...] = acc

    @pl.when(g == n_pages // IDX_PAGES - 1)
    def _():
        past = sc_ref[...]
        row = lax.broadcasted_iota(I32, (PAGE, 1), 0)
        new = jnp.where(row == 0, col_score(iknew_ref[0]), -jnp.inf)
        newk = new

        def count_ge(cand):
            hits = jnp.sum(jnp.where(past >= cand, 1, 0).astype(I32), axis=0, keepdims=True)
            hits = jnp.sum(hits, axis=1, keepdims=True)
            return hits + jnp.sum(jnp.where(newk >= cand, 1, 0).astype(I32), axis=0, keepdims=True)

        thr = _kth_largest(count_ge, topk, (1, 1))
        sel = jnp.where(past >= thr, 1.0, 0.0).astype(BF16)
        r = lax.broadcasted_iota(I32, (PAGE_ROWS, PAGE), 0)
        k = lax.broadcasted_iota(I32, (PAGE_ROWS, PAGE), 1)
        expand = jnp.where(r // A_KV == k, 1.0, 0.0).astype(BF16)
        sel_ref[0] = jnp.dot(expand, sel, preferred_element_type=F32).astype(sel_ref.dtype)
        new_sel = jnp.where(newk[0:1, :] >= thr, 1.0, 0.0)
        rr = lax.broadcasted_iota(I32, (PAGE_ROWS, LANE), 0)
        selnew_ref[0] = jnp.where(rr < A_KV, new_sel, 0.0)


def _dsa_sample_attn_kernel(pt_ref, *refs, n_pages):
    k_refs = refs[:ATT_PAGES]
    v_refs = refs[ATT_PAGES:2 * ATT_PAGES]
    qt_ref, sel_ref, selnew_ref, knew_ref, vnew_ref, o_ref, m_ref, l_ref, acc_ref = refs[2 * ATT_PAGES:]
    g = pl.program_id(1)

    @pl.when(g == 0)
    def _():
        m_ref[...] = jnp.full(m_ref.shape, NEG_BIG, F32)
        l_ref[...] = jnp.zeros_like(l_ref)
        acc_ref[...] = jnp.zeros_like(acc_ref)

    qt = qt_ref[0]
    r = lax.broadcasted_iota(I32, (PAGE_ROWS, LANE), 0)
    lane = lax.broadcasted_iota(I32, (PAGE_ROWS, LANE), 1)
    head_bias = jnp.where(r % A_KV == lane // A_GROUP, 0.0, NEG_BIG)
    pick_r = lax.broadcasted_iota(I32, (PAGE, LANE), 0)
    tn = (((0,), (0,)), ((), ()))

    def step(pages):
        s = [jnp.dot(k.astype(BF16), qt, preferred_element_type=F32) + head_bias + (sel_b - 1.0) * (-NEG_BIG)
             for k, _, sel_b in pages]
        m_old = m_ref[...]
        m_new = m_old
        for sc in s:
            m_new = jnp.maximum(m_new, jnp.max(sc, axis=0, keepdims=True))
        alpha = jnp.exp2(m_old - m_new)
        l = alpha * l_ref[...]
        acc = alpha * acc_ref[...]
        for sc, (_, v, _) in zip(s, pages):
            p = jnp.exp2(sc - m_new)
            l = l + jnp.sum(p, axis=0, keepdims=True)
            acc = acc + lax.dot_general(v.astype(BF16), p.astype(BF16), tn, preferred_element_type=F32)
        l_ref[...] = l
        acc_ref[...] = acc
        m_ref[...] = m_new

    sel = sel_ref[0]
    pages = []
    for j in range(ATT_PAGES):
        onehot = jnp.where(pick_r == g * ATT_PAGES + j, 1.0, 0.0).astype(BF16)
        sel_b = jnp.dot(sel, onehot, preferred_element_type=F32)
        pages.append((k_refs[j][0], v_refs[j][0], sel_b))
    step(pages)

    @pl.when(g == n_pages // ATT_PAGES - 1)
    def _():
        step([(knew_ref[0], vnew_ref[0], selnew_ref[0])])
        o_ref[0] = acc_ref[...] / l_ref[...]


def _dsa_sample(aq, ak, av, iq, iw, ik, pool_k, pool_v, pool_kidx, page_table):
    bd, n_pages = page_table.shape
    n_pool = pool_k.shape[0]
    topk = min(TOPK_MAX, (n_pages * PAGE + 1) // 4)
    assert n_pages == LANE and n_pages % IDX_PAGES == 0 and n_pages % ATT_PAGES == 0
    iqt = jnp.zeros((bd, I_DIM, LANE), BF16).at[:, :, :I_HEADS].set(
        iq.reshape(bd, I_HEADS, I_DIM).transpose(0, 2, 1))
    w_row = jnp.where(jnp.arange(LANE) < I_HEADS, iw, 0.0).reshape(bd, 1, LANE)
    ik_new = jnp.zeros((bd, PAGE, I_DIM), F32).at[:, 0, :].set(ik)
    new_rows = lambda a: jnp.zeros((bd, PAGE_ROWS, A_DH), F32).at[:, :A_KV, :].set(a.reshape(bd, A_KV, A_DH))
    k_new, v_new = new_rows(ak), new_rows(av)
    qt = jnp.zeros((bd, A_DH, LANE), BF16).at[:, :, :A_HEADS].set(
        aq.reshape(bd, A_HEADS, A_DH).transpose(0, 2, 1))

    def page_spec(rows, width, j, per):
        return pl.BlockSpec((1, rows, width), lambda b, g, pt: (pt[b, g * per + j], 0, 0))

    per_row = lambda shape: pl.BlockSpec((1,) + shape, lambda b, g, pt: (b, 0, 0))
    sel, sel_new = pl.pallas_call(
        functools.partial(_dsa_sample_index_kernel, n_pages=n_pages, topk=topk),
        grid_spec=pltpu.PrefetchScalarGridSpec(
            num_scalar_prefetch=1,
            grid=(bd, n_pages // IDX_PAGES),
            in_specs=[page_spec(PAGE, I_DIM, j, IDX_PAGES) for j in range(IDX_PAGES)]
            + [per_row((I_DIM, LANE)), per_row((1, LANE)), per_row((PAGE, I_DIM))],
            out_specs=[per_row((PAGE_ROWS, PAGE)), per_row((PAGE_ROWS, LANE))],
            scratch_shapes=[pltpu.VMEM((PAGE, LANE), F32)],
        ),
        out_shape=[jax.ShapeDtypeStruct((bd, PAGE_ROWS, PAGE), BF16),
                   jax.ShapeDtypeStruct((bd, PAGE_ROWS, LANE), F32)],
        compiler_params=_cparams(("parallel", "arbitrary")),
        name="dsa_sample_index",
    )(page_table, *([pool_kidx] * IDX_PAGES), iqt, w_row, ik_new)

    pk = pool_k.reshape(n_pool, PAGE_ROWS, A_DH)
    pv = pool_v.reshape(n_pool, PAGE_ROWS, A_DH)
    o_t = pl.pallas_call(
        functools.partial(_dsa_sample_attn_kernel, n_pages=n_pages),
        grid_spec=pltpu.PrefetchScalarGridSpec(
            num_scalar_prefetch=1,
            grid=(bd, n_pages // ATT_PAGES),
            in_specs=[page_spec(PAGE_ROWS, A_DH, j, ATT_PAGES) for j in range(ATT_PAGES)] * 2
            + [per_row((A_DH, LANE)), per_row((PAGE_ROWS, PAGE)), per_row((PAGE_ROWS, LANE)),
               per_row((PAGE_ROWS, A_DH)), per_row((PAGE_ROWS, A_DH))],
            out_specs=per_row((A_DH, LANE)),
            scratch_shapes=[pltpu.VMEM((1, LANE), F32), pltpu.VMEM((1, LANE), F32), pltpu.VMEM((A_DH, LANE), F32)],
        ),
        out_shape=jax.ShapeDtypeStruct((bd, A_DH, LANE), F32),
        compiler_params=_cparams(("parallel", "arbitrary")),
        name="dsa_sample_attn",
    )(page_table, *([pk] * ATT_PAGES), *([pv] * ATT_PAGES), qt, sel, sel_new, k_new, v_new)
    return o_t[:, :, :A_HEADS].transpose(0, 2, 1).reshape(bd, A_HEADS * A_DH).astype(BF16)


def _gate_mix_kernel(rn_ref, att_ref, wr_ref, wa_ref, gr_ref, ga_ref, o_ref):
    r = jnp.dot(rn_ref[...], wr_ref[...], preferred_element_type=F32)
    a = jnp.dot(att_ref[...], wa_ref[...], preferred_element_type=F32)
    o_ref[...] = (gr_ref[...].astype(F32) * r + ga_ref[...].astype(F32) * a).astype(o_ref.dtype)


def _out_proj_kernel(x_ref, mix_ref, wo_ref, nw_ref, h_ref, hn_ref):
    h = x_ref[...] + jnp.dot(mix_ref[...], wo_ref[...], preferred_element_type=F32)
    h_ref[...] = h
    ms = jnp.mean(h * h, axis=-1, keepdims=True)
    hn_ref[...] = (h * lax.rsqrt(ms + EPS) * nw_ref[...]).astype(hn_ref.dtype)


def _mixer_out(x2d, rn, att, gates, wr, wa, wo, norm2_w, tm):
    n, d = x2d.shape
    tn = 512
    ncol = d // tn
    mixed = pl.pallas_call(
        _gate_mix_kernel,
        grid=(ncol, n // tm),
        in_specs=[
            pl.BlockSpec((tm, rn.shape[1]), lambda j, i: (i, 0)),
            pl.BlockSpec((tm, att.shape[1]), lambda j, i: (i, 0)),
            pl.BlockSpec((wr.shape[0], tn), lambda j, i: (0, j)),
            pl.BlockSpec((wa.shape[0], tn), lambda j, i: (0, j)),
            pl.BlockSpec((tm, tn), lambda j, i: (i, j)),
            pl.BlockSpec((tm, tn), lambda j, i: (i, ncol + j)),
        ],
        out_specs=pl.BlockSpec((tm, tn), lambda j, i: (i, j)),
        out_shape=jax.ShapeDtypeStruct((n, d), BF16),
        compiler_params=_cparams(("parallel", "parallel")),
        name="gate_mix",
    )(rn, att, wr, wa, gates, gates)
    tm2 = min(tm, 256)
    h, hn = pl.pallas_call(
        _out_proj_kernel,
        grid=(n // tm2,),
        in_specs=[
            pl.BlockSpec((tm2, d), lambda i: (i, 0)),
            pl.BlockSpec((tm2, d), lambda i: (i, 0)),
            pl.BlockSpec((d, d), lambda i: (0, 0)),
            pl.BlockSpec((1, d), lambda i: (0, 0)),
        ],
        out_specs=[pl.BlockSpec((tm2, d), lambda i: (i, 0)), pl.BlockSpec((tm2, d), lambda i: (i, 0))],
        out_shape=[jax.ShapeDtypeStruct((n, d), F32), jax.ShapeDtypeStruct((n, d), BF16)],
        compiler_params=_cparams(("parallel",)),
        name="out_proj",
    )(x2d, mixed, wo, norm2_w.reshape(1, d))
    return h, hn


PEER_TE = 1024
NO_RANK = 127.0


def _top_desc(x, count):
    vals = []
    cur = x
    for _ in range(count):
        mx = jnp.max(cur, axis=0, keepdims=True)
        vals.append(mx)
        cur = jnp.where(cur == mx, -jnp.inf, cur)
    return vals


def _top_desc_rank(x, count):
    vals = []
    cur = x
    rank = jnp.full(x.shape, NO_RANK, F32)
    for i in range(count):
        mx = jnp.max(cur, axis=0, keepdims=True)
        hit = cur == mx
        vals.append(mx)
        rank = jnp.where(hit, float(i), rank)
        cur = jnp.where(hit, -jnp.inf, cur)
    return vals, rank


def _peer_route_kernel(q_ref, k1_ref, k2_ref, n1_ref, e1_ref, r2_ref, e2_ref):
    nt = (((1,), (1,)), ((), ()))
    half = P_TOPK // 2
    for h in range(P_HEADS):
        qh = q_ref[:, h * P_DKEY:(h + 1) * P_DKEY]
        s1 = lax.dot_general(k1_ref[h], qh, nt, preferred_element_type=F32)
        s2 = lax.dot_general(k2_ref[h], qh, nt, preferred_element_type=F32)
        v1, r1 = _top_desc_rank(s1, P_TOPK)
        v2, r2 = _top_desc_rank(s2, P_TOPK)
        v1blk = jnp.concatenate(v1, axis=0)
        v2blk = jnp.concatenate(v2, axis=0)
        blocks = ([v1[0] + v2blk] + [v1[i] + v2blk[:half] for i in range(1, half)]
                  + [v1blk[half:] + v2[0]])
        c = _top_desc(jnp.concatenate(blocks, axis=0), P_TOPK)
        thr = c[-1]
        z = jnp.zeros_like(thr)
        for i in range(P_TOPK):
            z = z + jnp.exp(c[i] - c[0])
        counts = [jnp.sum(jnp.where(blk >= thr, 1.0, 0.0), axis=0, keepdims=True) for blk in blocks[:half]]
        tail = jnp.where(blocks[half] >= thr, 1.0, 0.0)
        counts += [tail[i:i + 1] for i in range(half)]
        n1 = jnp.zeros_like(s1)
        for i in range(P_TOPK):
            n1 = jnp.where(r1 == float(i), counts[i], n1)
        n1_ref[h] = n1
        e1_ref[h] = jnp.where(r1 < P_TOPK, jnp.exp(s1 - v1[0]) / z, 0.0)
        r2_ref[h] = r2.astype(r2_ref.dtype)
        e2_ref[h] = jnp.where(r2 < P_TOPK, jnp.exp(s2 - v2[0]), 0.0).astype(e2_ref.dtype)


def _peer_expert_kernel(hn_ref, h_ref, u_ref, vt_ref, n1_ref, e1_ref, r2_ref, e2_ref,
                        y_ref, acc_ref, w_ref, *, a_per_tile):
    j = pl.program_id(1)

    @pl.when(j == 0)
    def _():
        acc_ref[...] = jnp.zeros_like(acc_ref)

    tm = w_ref.shape[1]
    for al in range(a_per_tile):
        a = j * a_per_tile + al
        w = jnp.zeros((P_NKEYS, tm), BF16)
        for h in range(P_HEADS):
            nrow = jnp.broadcast_to(n1_ref[h, pl.ds(a, 1), :].astype(BF16), (P_NKEYS, tm))
            erow = jnp.broadcast_to(e1_ref[h, pl.ds(a, 1), :].astype(BF16), (P_NKEYS, tm))
            w = w + jnp.where(r2_ref[h] < nrow, e2_ref[h], jnp.zeros((), BF16)) * erow
        w_ref[al * P_NKEYS:(al + 1) * P_NKEYS, :] = w

    nt = (((1,), (1,)), ((), ()))
    x = lax.dot_general(u_ref[...], hn_ref[...], nt, preferred_element_type=F32)
    p = w_ref[...] * (0.5 * x * (1.0 + lax.erf(x * (2.0 ** -0.5)))).astype(BF16)
    acc_ref[...] += jnp.dot(vt_ref[...], p, preferred_element_type=F32)

    @pl.when(j == pl.num_programs(1) - 1)
    def _():
        y_ref[...] = h_ref[...] + acc_ref[...].T


def _peer(hn, h, wq, k1p, k2p, u_bf, vt_bf, tm):
    n, d = hn.shape
    te = PEER_TE
    hk = P_HEADS * P_DKEY
    tr = min(tm, 128)
    tok = lambda i: (0, 0, i)
    head_blk = pl.BlockSpec((P_HEADS, P_NKEYS, tr), tok)
    unused = jnp.zeros((tm, LANE), F32)
    q, = _proj(hn, wq, (0, hk), unused, unused, mode="plain", out_dtypes=(BF16,), tm=tm, tn=512)
    n1, e1, r2, e2 = pl.pallas_call(
        _peer_route_kernel,
        grid=(n // tr,),
        in_specs=[
            pl.BlockSpec((tr, hk), lambda i: (i, 0)),
            pl.BlockSpec((P_HEADS, P_NKEYS, P_DKEY), lambda i: (0, 0, 0)),
            pl.BlockSpec((P_HEADS, P_NKEYS, P_DKEY), lambda i: (0, 0, 0)),
        ],
        out_specs=[head_blk] * 4,
        out_shape=[jax.ShapeDtypeStruct((P_HEADS, P_NKEYS, n), dt) for dt in (F32, F32, BF16, BF16)],
        compiler_params=_cparams(("parallel",)),
        name="peer_route",
    )(q, k1p, k2p)

    n_exp = u_bf.shape[0]
    once = pl.Buffered(1)
    tok2 = lambda i, j: (0, 0, i)
    head_f32 = pl.BlockSpec((P_HEADS, P_NKEYS, tm), tok2, pipeline_mode=once)
    head_bf16 = pl.BlockSpec((P_HEADS, P_NKEYS, tm), tok2)
    return pl.pallas_call(
        functools.partial(_peer_expert_kernel, a_per_tile=te // P_NKEYS),
        grid=(n // tm, n_exp // te),
        in_specs=[
            pl.BlockSpec((tm, d), lambda i, j: (i, 0)),
            pl.BlockSpec((tm, d), lambda i, j: (i, 0), pipeline_mode=once),
            pl.BlockSpec((te, d), lambda i, j: (j, 0)),
            pl.BlockSpec((d, te), lambda i, j: (0, j)),
            head_f32, head_f32, head_bf16, head_bf16,
        ],
        out_specs=pl.BlockSpec((tm, d), lambda i, j: (i, 0)),
        out_shape=jax.ShapeDtypeStruct((n, d), F32),
        scratch_shapes=[pltpu.VMEM((d, tm), F32), pltpu.VMEM((te, tm), BF16)],
        compiler_params=_cparams(("parallel", "arbitrary")),
        name="peer_experts",
    )(hn, h, u_bf, vt_bf, n1, e1, r2, e2)


def _prep_weights(w_in, w_ret_o, w_att_o, w_out, peer_wq, peer_keys1, peer_keys2, peer_u, peer_v):
    half = P_DKEY // 2
    k1p = jnp.pad(peer_keys1, ((0, 0), (0, 0), (0, half))).astype(BF16)
    k2p = jnp.pad(peer_keys2, ((0, 0), (0, 0), (half, 0))).astype(BF16)
    return dict(w_in=w_in, w_gate=w_in[:, GATE_COL:GATE_COL + GATE_WIDTH],
                wr=w_ret_o.astype(BF16), wa=w_att_o.astype(BF16), wo=w_out.astype(BF16),
                wq=peer_wq, k1p=k1p, k2p=k2p,
                u_bf=peer_u.astype(BF16), vt_bf=peer_v.T.astype(BF16))


def _prompt_layer(x, W, norm1_w, q_norm_w, k_norm_w, ret_gn_w, norm2_w):
    bsz, seq, d = x.shape
    x2d = x.reshape(bsz * seq, d)
    tm = 512
    p = _mixer_in(x2d, jnp.arange(seq), norm1_w, W["w_in"], W["w_gate"], q_norm_w, k_norm_w, 2 * tm)
    s0 = jnp.zeros((bsz, R_HEADS, R_DK, R_DV), F32)
    rn, S = _retention_prompt(p["rq"], p["rk"], p["rvg"], ret_gn_w, s0, bsz, seq)
    att = _dsa_prompt(p["iq"], p["iw"], p["ik_bf"], p["aq"], p["ak_bf"], p["av_bf"], bsz, seq)
    h, hn = _mixer_out(x2d, rn, att, p["gates"], W["wr"], W["wa"], W["wo"], norm2_w, tm)
    y = _peer(hn, h, W["wq"], W["k1p"], W["k2p"], W["u_bf"], W["vt_bf"], tm=512)
    return (y.reshape(bsz, seq, d), p["ak"].reshape(bsz, seq, A_KV, A_DH), p["av"].reshape(bsz, seq, A_KV, A_DH),
            p["ik"].reshape(bsz, seq, I_DIM), S)


def _sample_layer(x, pool_k, pool_v, pool_kidx, s_in, page_table, W, norm1_w, q_norm_w, k_norm_w,
                  ret_gn_w, norm2_w):
    bd, ts, d = x.shape
    assert ts == 1
    x2d = x.reshape(bd, d)
    past = page_table.shape[1] * PAGE
    p = _mixer_in(x2d, jnp.full((bd,), past, I32), norm1_w, W["w_in"], W["w_gate"], q_norm_w, k_norm_w, bd)
    rn, S = _retention_sample(p["rq"], p["rk"], p["rvg"], ret_gn_w, s_in)
    att = _dsa_sample(p["aq"], p["ak"], p["av"], p["iq"], p["iw"], p["ik"], pool_k, pool_v, pool_kidx, page_table)
    h, hn = _mixer_out(x2d, rn, att, p["gates"], W["wr"], W["wa"], W["wo"], norm2_w, bd)
    pad = LANE - bd
    hn_p = jnp.pad(hn, ((0, pad), (0, 0)))
    h_p = jnp.pad(h, ((0, pad), (0, 0)))
    y = _peer(hn_p, h_p, W["wq"], W["k1p"], W["k2p"], W["u_bf"], W["vt_bf"], tm=LANE)[:bd]
    return (y.reshape(bd, ts, d), p["ak"].reshape(bd, ts, A_KV, A_DH), p["av"].reshape(bd, ts, A_KV, A_DH),
            p["ik"].reshape(bd, ts, I_DIM), S)


def kernel(x_prompt, x_sample, cache_k, cache_v, cache_k_idx, state_ret, page_table, norm1_w, w_in, q_norm_w,
           k_norm_w, ret_gn_w, w_ret_o, w_att_o, w_out, norm2_w, peer_wq, peer_keys1, peer_keys2, peer_u, peer_v):
    depth = w_in.shape[0]
    assert depth == 1
    l = 0
    W = _prep_weights(w_in[l], w_ret_o[l], w_att_o[l], w_out[l], peer_wq[l], peer_keys1[l], peer_keys2[l],
                      peer_u[l], peer_v[l])
    yp, kp, vp, ikp, sp = _prompt_layer(x_prompt, W, norm1_w[l], q_norm_w[l], k_norm_w[l], ret_gn_w[l], norm2_w[l])
    ys, kn, vn, ikn, sn = _sample_layer(x_sample, cache_k[l], cache_v[l], cache_k_idx[l], state_ret[l], page_table,
                                        W, norm1_w[l], q_norm_w[l], k_norm_w[l], ret_gn_w[l], norm2_w[l])
    return (yp, ys, kp[None], vp[None], ikp[None], sp[None], kn[None], vn[None], ikn[None], sn[None])
```

```python
import functools

import jax
import jax.numpy as jnp
from jax import lax
from jax.experimental import pallas as pl
from jax.experimental.pallas import tpu as pltpu

F32 = jnp.float32
BF16 = jnp.bfloat16
I32 = jnp.int32

D_MODEL = 2048
PAGE = 128
R_HEADS, R_DK, R_DV, R_CHUNK = 8, 128, 256, 128
A_HEADS, A_KV, A_DH = 16, 4, 128
A_GROUP = A_HEADS // A_KV
I_HEADS, I_DIM = 16, 128
TOPK_MAX = 256
Q_BLOCK = 128
ROPE_THETA = 10000.0
P_HEADS, P_NKEYS, P_DKEY, P_TOPK = 8, 128, 128, 16
EPS = 1e-6

LANE = 128
VMEM_LIMIT = 56 * 1024 * 1024
NEG_BIG = -1e30
LOG2E = 1.4426950408889634
INT_MIN = -2147483648
NEGINF_KEY = INT_MIN + 0x7FFFFF

SEG = dict(rq=(0, 1024), rk=(1024, 1024), rvg=(2048, 4096), aq=(6144, 2048), ak=(8192, 512), av=(8704, 512),
           iq=(9216, 2048), ik=(11264, 128), iw=(11392, 128))
GATE_COL = 11408
GATE_WIDTH = 2 * D_MODEL


def _cparams(sem):
    return pltpu.CompilerParams(dimension_semantics=sem, vmem_limit_bytes=VMEM_LIMIT)


def _rmsnorm_kernel(x_ref, w_ref, o_ref):
    x = x_ref[...]
    ms = jnp.mean(x * x, axis=-1, keepdims=True)
    o_ref[...] = (x * lax.rsqrt(ms + EPS) * w_ref[...]).astype(o_ref.dtype)


def _rmsnorm(x, w, tm):
    n, d = x.shape
    return pl.pallas_call(
        _rmsnorm_kernel,
        grid=(n // tm,),
        in_specs=[pl.BlockSpec((tm, d), lambda i: (i, 0)), pl.BlockSpec((1, d), lambda i: (0, 0))],
        out_specs=pl.BlockSpec((tm, d), lambda i: (i, 0)),
        out_shape=jax.ShapeDtypeStruct((n, d), BF16),
        compiler_params=_cparams(("parallel",)),
        name="rmsnorm",
    )(x, w.reshape(1, d))


def _proj_kernel(x_ref, w_ref, a_ref, b_ref, *refs, mode, scale, heads):
    out_refs, wb_ref = refs[:-1], refs[-1]

    @pl.when(pl.program_id(1) == 0)
    def _():
        wb_ref[...] = w_ref[...].astype(wb_ref.dtype)

    z = lax.dot_general(x_ref[...], wb_ref[...], (((1,), (1,)), ((), ())), preferred_element_type=F32)
    if mode in ("rope", "norm_rope"):
        a, b = a_ref[...], b_ref[...]
        parts = []
        for h in range(heads):
            zh = z[:, h * LANE:(h + 1) * LANE]
            y = zh * a + pltpu.roll(zh, LANE // 2, 1) * b
            if mode == "norm_rope":
                sq = zh * zh
                hi = sq.astype(BF16)
                lo = (sq - hi.astype(F32)).astype(BF16)
                avg = jnp.full((LANE, LANE), 1.0 / LANE, BF16)
                ms = (jnp.dot(hi, avg, preferred_element_type=F32) + jnp.dot(lo, avg, preferred_element_type=F32))
                y = y * lax.rsqrt(ms + EPS)
            parts.append(y)
        z = jnp.concatenate(parts, axis=1) if heads > 1 else parts[0]
    elif mode == "sigmoid":
        z = jax.nn.sigmoid(z)
    elif scale != 1.0:
        z = z * scale
    for o in out_refs:
        if len(o.shape) == 3:
            for h in range(heads):
                o[:, h, :] = z[:, h * LANE:(h + 1) * LANE].astype(o.dtype)
        else:
            o[...] = z.astype(o.dtype)


def _proj(xn, w_t, seg, a_tab, b_tab, *, mode, scale=1.0, out_dtypes, tm, tn, split_first=False):
    n, d = xn.shape
    col0, width = seg
    tn = min(tn, width)
    assert col0 % 8 == 0 and width % tn == 0 and n % tm == 0 and not (split_first and tn != width)
    pos_blocks = a_tab.shape[0] // tm
    out_blocks = [pl.BlockSpec((tm, tn), lambda j, i: (i, j)) for _ in out_dtypes]
    out_shapes = [jax.ShapeDtypeStruct((n, width), dt) for dt in out_dtypes]
    if split_first:
        out_blocks[0] = pl.BlockSpec((tm, width // LANE, LANE), lambda j, i: (i, 0, 0))
        out_shapes[0] = jax.ShapeDtypeStruct((n, width // LANE, LANE), out_dtypes[0])
    kern = functools.partial(_proj_kernel, mode=mode, scale=scale, heads=tn // LANE)
    outs = pl.pallas_call(
        kern,
        grid=(width // tn, n // tm),
        in_specs=[
            pl.BlockSpec((tm, d), lambda j, i: (i, 0)),
            pl.BlockSpec((pl.Element(tn), pl.Element(d)), lambda j, i: (pl.multiple_of(col0 + j * tn, 8), 0)),
            pl.BlockSpec((tm, LANE), lambda j, i: (i % pos_blocks, 0)),
            pl.BlockSpec((tm, LANE), lambda j, i: (i % pos_blocks, 0)),
        ],
        out_specs=out_blocks,
        out_shape=out_shapes,
        scratch_shapes=[pltpu.VMEM((tn, d), BF16)],
        compiler_params=_cparams(("parallel", "arbitrary")),
        name="proj_" + mode,
    )(xn, w_t, a_tab, b_tab)
    return outs


def _rope_tables(pos):
    half = LANE // 2
    freqs = ROPE_THETA ** (-jnp.arange(half, dtype=F32) / half)
    ang = pos.astype(F32)[:, None] * freqs[None, :]
    cos, sin = jnp.cos(ang), jnp.sin(ang)
    return jnp.concatenate([cos, cos], axis=1), jnp.concatenate([-sin, sin], axis=1)


def _mixer_in(x2d, pos_rows, norm1_w, w_t, q_norm_w, k_norm_w, tm):
    xn = _rmsnorm(x2d, norm1_w, tm)
    cos2, sin2 = _rope_tables(pos_rows)
    swap = lambda g: jnp.roll(g, LANE // 2)
    rope = lambda gain, scale: (cos2 * (gain * scale)[None, :], sin2 * (swap(gain) * scale)[None, :])
    ones = jnp.ones((LANE,), F32)
    p = functools.partial(_proj, xn, tm=tm, tn=512)
    out = {}
    out["rq"], = p(w_t, SEG["rq"], *rope(ones, 1.0), mode="rope", out_dtypes=(BF16,))
    out["rk"], = p(w_t, SEG["rk"], *rope(ones, R_DK ** -0.5), mode="rope", out_dtypes=(BF16,))
    out["rvg"], = p(w_t, SEG["rvg"], cos2, sin2, mode="plain", out_dtypes=(BF16,))
    out["aq"], = p(w_t, SEG["aq"], *rope(q_norm_w, A_DH ** -0.5 * LOG2E), mode="norm_rope", out_dtypes=(BF16,))
    out["ak"], out["ak_bf"] = p(w_t, SEG["ak"], *rope(k_norm_w, 1.0), mode="norm_rope", out_dtypes=(F32, BF16),
                                 split_first=True)
    out["av"], out["av_bf"] = p(w_t, SEG["av"], cos2, sin2, mode="plain", out_dtypes=(F32, BF16), split_first=True)
    out["iq"], = p(w_t, SEG["iq"], *rope(ones, I_DIM ** -0.5), mode="rope", out_dtypes=(BF16,))
    out["ik"], out["ik_bf"] = p(w_t, SEG["ik"], *rope(ones, 1.0), mode="rope", out_dtypes=(F32, BF16))
    out["iw"], = p(w_t, SEG["iw"], cos2, sin2, mode="plain", scale=I_HEADS ** -0.5, out_dtypes=(F32,))
    out["gates"], = p(w_t, (GATE_COL, GATE_WIDTH), cos2, sin2, mode="sigmoid", out_dtypes=(BF16,))
    return out


def _ret_tables(chunk):
    log_g = jnp.log1p(-jnp.exp2(-5.0 - jnp.arange(R_HEADS, dtype=F32)))
    n = jnp.arange(chunk, dtype=F32)
    diff = n[:, None] - n[None, :]
    causal = diff >= 0
    dmat = jnp.where(causal, jnp.exp(jnp.where(causal, diff, 0.0)[None] * log_g[:, None, None]), 0.0)
    qdec = jnp.exp((n + 1.0)[None, :] * log_g[:, None])
    kdec = jnp.exp((chunk - 1.0 - n)[None, :] * log_g[:, None])
    gc = jnp.exp(chunk * log_g)
    return dmat, qdec, kdec, gc


def _ret_kernel(q_ref, k_ref, v_ref, g_ref, gnw_ref, dmat_ref, qdec_ref, kdec_ref, gc_ref, s0_ref,
                o_ref, s_ref):
    c = pl.program_id(1)

    @pl.when(c == 0)
    def _():
        s_ref[...] = s0_ref[...]

    nt = (((1,), (1,)), ((), ()))
    tn = (((0,), (0,)), ((), ()))
    for h in range(R_HEADS):
        S = s_ref[0, h]
        q = q_ref[:, h * R_DK:(h + 1) * R_DK]
        k = k_ref[:, h * R_DK:(h + 1) * R_DK]
        v = v_ref[:, h * R_DV:(h + 1) * R_DV]
        scores = lax.dot_general(q, k, nt, preferred_element_type=F32) * dmat_ref[h]
        intra = jnp.dot(scores.astype(BF16), v, preferred_element_type=F32)
        qd = (q.astype(F32) * qdec_ref[h]).astype(BF16)
        cross = jnp.dot(qd, S.astype(BF16), preferred_element_type=F32)
        kd = (k.astype(F32) * kdec_ref[h]).astype(BF16)
        s_ref[0, h] = gc_ref[h] * S + lax.dot_general(kd, v, tn, preferred_element_type=F32)
        o = intra + cross
        mu = jnp.mean(o, axis=-1, keepdims=True)
        var = jnp.mean(jnp.square(o - mu), axis=-1, keepdims=True)
        g = g_ref[:, h * R_DV:(h + 1) * R_DV].astype(F32)
        rn = (o - mu) * lax.rsqrt(var + EPS) * gnw_ref[:, h * R_DV:(h + 1) * R_DV] * (g * jax.nn.sigmoid(g))
        o_ref[:, h * R_DV:(h + 1) * R_DV] = rn.astype(o_ref.dtype)


def _retention_prompt(rq, rk, rvg, gn_w, s0, bsz, seq):
    C = R_CHUNK
    nc = seq // C
    dmat, qdec, kdec, gc = _ret_tables(C)
    qdec = jnp.broadcast_to(qdec[:, :, None], (R_HEADS, C, R_DK))
    kdec = jnp.broadcast_to(kdec[:, :, None], (R_HEADS, C, R_DK))
    gc = jnp.broadcast_to(gc[:, None, None], (R_HEADS, 1, R_DV))
    hk, hv = R_HEADS * R_DK, R_HEADS * R_DV
    row = lambda b, c: (b * nc + c, 0)
    const3 = lambda b, c: (0, 0, 0)
    state = pl.BlockSpec((1, R_HEADS, R_DK, R_DV), lambda b, c: (b, 0, 0, 0))
    rn, S = pl.pallas_call(
        _ret_kernel,
        grid=(bsz, nc),
        in_specs=[
            pl.BlockSpec((C, hk), row),
            pl.BlockSpec((C, hk), row),
            pl.BlockSpec((C, hv), row),
            pl.BlockSpec((C, hv), lambda b, c: (b * nc + c, 1)),
            pl.BlockSpec((1, hv), lambda b, c: (0, 0)),
            pl.BlockSpec((R_HEADS, C, C), const3),
            pl.BlockSpec((R_HEADS, C, R_DK), const3),
            pl.BlockSpec((R_HEADS, C, R_DK), const3),
            pl.BlockSpec((R_HEADS, 1, R_DV), const3),
            state,
        ],
        out_specs=[pl.BlockSpec((C, hv), row), state],
        out_shape=[jax.ShapeDtypeStruct((bsz * seq, hv), BF16),
                   jax.ShapeDtypeStruct((bsz, R_HEADS, R_DK, R_DV), F32)],
        compiler_params=_cparams(("parallel", "arbitrary")),
        name="retention",
    )(rq, rk, rvg, rvg, gn_w.reshape(1, -1), dmat, qdec, kdec, gc, s0)
    return rn, S


def _ret_step_kernel(qc_ref, kc_ref, v_ref, g_ref, gnw_ref, gam_ref, s0_ref, o_ref, s_ref):
    for h in range(R_HEADS):
        S = s0_ref[0, h]
        qc, kc = qc_ref[0, h], kc_ref[0, h]
        v = v_ref[0, h]
        gam = gam_ref[h]
        qk = jnp.sum(qc * kc, axis=0, keepdims=True)
        cross = jnp.sum((qc * gam) * S, axis=0, keepdims=True)
        o = qk * v + cross
        s_ref[0, h] = gam * S + kc * v
        mu = jnp.mean(o, axis=-1, keepdims=True)
        var = jnp.mean(jnp.square(o - mu), axis=-1, keepdims=True)
        g = g_ref[0, h]
        o_ref[0, h] = (o - mu) * lax.rsqrt(var + EPS) * gnw_ref[0, h] * (g * jax.nn.sigmoid(g))


def _retention_sample(rq, rk, rvg, gn_w, s0):
    bd = rq.shape[0]
    _, qdec, _, _ = _ret_tables(1)
    gam = jnp.broadcast_to(qdec[:, :, None], (R_HEADS, 1, R_DV)).astype(F32)
    qc = rq.astype(F32).reshape(bd, R_HEADS, R_DK, 1)
    kc = rk.astype(F32).reshape(bd, R_HEADS, R_DK, 1)
    rv = rvg[:, :R_HEADS * R_DV].astype(F32).reshape(bd, R_HEADS, 1, R_DV)
    rg = rvg[:, R_HEADS * R_DV:].astype(F32).reshape(bd, R_HEADS, 1, R_DV)
    gnw = gn_w.reshape(1, R_HEADS, 1, R_DV)
    col = pl.BlockSpec((1, R_HEADS, R_DK, 1), lambda b: (b, 0, 0, 0))
    rowv = pl.BlockSpec((1, R_HEADS, 1, R_DV), lambda b: (b, 0, 0, 0))
    st = pl.BlockSpec((1, R_HEADS, R_DK, R_DV), lambda b: (b, 0, 0, 0))
    rn, S = pl.pallas_call(
        _ret_step_kernel,
        grid=(bd,),
        in_specs=[col, col, rowv, rowv,
                  pl.BlockSpec((1, R_HEADS, 1, R_DV), lambda b: (0, 0, 0, 0)),
                  pl.BlockSpec((R_HEADS, 1, R_DV), lambda b: (0, 0, 0)), st],
        out_specs=[rowv, st],
        out_shape=[jax.ShapeDtypeStruct((bd, R_HEADS, 1, R_DV), F32),
                   jax.ShapeDtypeStruct((bd, R_HEADS, R_DK, R_DV), F32)],
        compiler_params=_cparams(("parallel",)),
        name="retention_step",
    )(qc, kc, rv, rg, gnw, gam, s0)
    return rn.reshape(bd, R_HEADS * R_DV).astype(BF16), S


def _key_to_float(key):
    return pltpu.bitcast(key ^ ((key >> 31) & 0x7FFFFFFF), F32)


def _kth_largest(count_ge, k, shape):
    cur = jnp.where(count_ge(jnp.zeros(shape, F32)) >= k, 0, INT_MIN).astype(I32)

    def body(i, cur):
        cand = cur | jnp.left_shift(jnp.int32(1), 30 - i)
        return jnp.where(count_ge(_key_to_float(cand)) >= k, cand, cur)

    key = lax.fori_loop(0, 31, body, cur)
    return jnp.where(key <= NEGINF_KEY, jnp.finfo(F32).min, _key_to_float(jnp.maximum(key, NEGINF_KEY + 1)))


KEY_UNIT = 512
TRIP_KEYS = 1024


def _dsa_prompt_kernel(iq_ref, iw_ref, ik_ref, aq_ref, ak_ref, av_ref, o_ref,
                       keys_ref, bias_ref, m_ref, acc_ref, *, topk):
    qb = pl.program_id(1)
    ntrips = (qb * Q_BLOCK + Q_BLOCK + TRIP_KEYS - 1) // TRIP_KEYS
    nt = (((1,), (1,)), ((), ()))

    iq = iq_ref[...]
    pair = lambda hp: jnp.concatenate([iq[:, (2 * hp) * LANE:(2 * hp + 1) * LANE],
                                       iq[:, (2 * hp + 1) * LANE:(2 * hp + 2) * LANE]], axis=0)
    rhs = [pair(hp) for hp in range(I_HEADS // 2)]
    w_t = iw_ref[...].T
    t_idx = qb * Q_BLOCK + lax.broadcasted_iota(I32, (Q_BLOCK, Q_BLOCK), 1)
    row_iota = lax.broadcasted_iota(I32, (Q_BLOCK, Q_BLOCK), 0)

    def score_trip(t, carry):
        for c in range(TRIP_KEYS // Q_BLOCK):
            off = pl.multiple_of(t * TRIP_KEYS + c * Q_BLOCK, Q_BLOCK)
            ikc = ik_ref[0, pl.ds(off, Q_BLOCK), :]
            sc = jnp.zeros((Q_BLOCK, Q_BLOCK), F32)
            for hp in range(I_HEADS // 2):
                prod = lax.dot_general(ikc, rhs[hp], nt, preferred_element_type=F32)
                sc = sc + (jnp.maximum(prod[:, :Q_BLOCK], 0.0) * w_t[2 * hp:2 * hp + 1, :]
                           + jnp.maximum(prod[:, Q_BLOCK:], 0.0) * w_t[2 * hp + 1:2 * hp + 2, :])
            sc = jnp.where(off + row_iota <= t_idx, sc, -jnp.inf)
            keys_ref[pl.ds(off, Q_BLOCK), :] = sc
        return carry

    lax.fori_loop(0, ntrips, score_trip, 0)

    def count_ge(cand):
        def body(t, acc):
            for c in range(TRIP_KEYS // KEY_UNIT):
                off = pl.multiple_of(t * TRIP_KEYS + c * KEY_UNIT, KEY_UNIT)
                blk = keys_ref[pl.ds(off, KEY_UNIT), :]
                hit = jnp.where(blk >= cand, 1, 0).astype(I32)
                acc = acc + jnp.sum(hit.reshape(KEY_UNIT // 8, 8, Q_BLOCK), axis=0)
            return acc
        acc = lax.fori_loop(0, ntrips, body, jnp.zeros((8, Q_BLOCK), I32))
        return jnp.sum(acc, axis=0, keepdims=True)

    thr = _kth_largest(count_ge, topk, (1, Q_BLOCK))

    def bias_trip(t, carry):
        for c in range(TRIP_KEYS // Q_BLOCK):
            off = pl.multiple_of(t * TRIP_KEYS + c * Q_BLOCK, Q_BLOCK)
            sel = keys_ref[pl.ds(off, Q_BLOCK), :] >= thr
            bias_ref[:, pl.ds(off, Q_BLOCK)] = jnp.where(sel, 0.0, NEG_BIG).astype(F32).T
        return carry

    lax.fori_loop(0, ntrips, bias_trip, 0)

    rows = A_GROUP * Q_BLOCK
    ones_v = jnp.ones((KEY_UNIT, A_DH), BF16)
    subs = TRIP_KEYS // KEY_UNIT

    def logits(n, off):
        qg = jnp.concatenate([aq_ref[:, (n * A_GROUP + g) * A_DH:(n * A_GROUP + g + 1) * A_DH]
                              for g in range(A_GROUP)], axis=0)
        kch = ak_ref[0, pl.ds(off, KEY_UNIT), n * A_DH:(n + 1) * A_DH]
        b = bias_ref[:, pl.ds(off, KEY_UNIT)]
        s = lax.dot_general(qg, kch, nt, preferred_element_type=F32)
        return s + jnp.concatenate([b] * A_GROUP, axis=0)

    m_ref[...] = jnp.full((A_KV, rows, LANE), NEG_BIG, F32)
    acc_ref[...] = jnp.zeros((A_KV, rows, 2 * A_DH), F32)

    def attend(t, carry):
        for n in range(A_KV):
            offs = [pl.multiple_of(t * TRIP_KEYS + c * KEY_UNIT, KEY_UNIT) for c in range(subs)]
            s = [logits(n, off) for off in offs]
            m_old = m_ref[n]
            m_new = m_old
            for sc in s:
                m_new = jnp.maximum(m_new, jnp.max(sc, axis=-1, keepdims=True))
            alpha = jnp.exp2(m_old - m_new)
            m = jnp.concatenate([m_new] * (KEY_UNIT // LANE), axis=1)
            pv = jnp.zeros((rows, 2 * A_DH), F32)
            for sc, off in zip(s, offs):
                p = jnp.exp2(sc - m).astype(BF16)
                vch = av_ref[0, pl.ds(off, KEY_UNIT), n * A_DH:(n + 1) * A_DH]
                pv = pv + jnp.dot(p, jnp.concatenate([vch, ones_v], axis=1), preferred_element_type=F32)
            acc_ref[n] = acc_ref[n] * jnp.concatenate([alpha, alpha], axis=1) + pv
            m_ref[n] = m_new
        return carry

    lax.fori_loop(0, ntrips, attend, 0)
    for n in range(A_KV):
        acc = acc_ref[n]
        o = acc[:, :A_DH] / acc[:, A_DH:]
        for g in range(A_GROUP):
            hh = n * A_GROUP + g
            o_ref[:, hh * A_DH:(hh + 1) * A_DH] = o[g * Q_BLOCK:(g + 1) * Q_BLOCK].astype(o_ref.dtype)


def _dsa_prompt(iq, iw, ik_bf, aq, ak_bf, av_bf, bsz, seq):
    nb = seq // Q_BLOCK
    topk = min(TOPK_MAX, seq // 4)
    assert seq % TRIP_KEYS == 0
    ik3 = ik_bf.reshape(bsz, seq, I_DIM)
    ak3 = ak_bf.reshape(bsz, seq, A_KV * A_DH)
    av3 = av_bf.reshape(bsz, seq, A_KV * A_DH)
    rowblk = lambda b, q: (b * nb + q, 0)
    rows = A_GROUP * Q_BLOCK
    return pl.pallas_call(
        functools.partial(_dsa_prompt_kernel, topk=topk),
        grid=(bsz, nb),
        in_specs=[
            pl.BlockSpec((Q_BLOCK, I_HEADS * I_DIM), rowblk),
            pl.BlockSpec((Q_BLOCK, LANE), rowblk),
            pl.BlockSpec((1, seq, I_DIM), lambda b, q: (b, 0, 0)),
            pl.BlockSpec((Q_BLOCK, A_HEADS * A_DH), rowblk),
            pl.BlockSpec((1, seq, A_KV * A_DH), lambda b, q: (b, 0, 0)),
            pl.BlockSpec((1, seq, A_KV * A_DH), lambda b, q: (b, 0, 0)),
        ],
        out_specs=pl.BlockSpec((Q_BLOCK, A_HEADS * A_DH), rowblk),
        out_shape=jax.ShapeDtypeStruct((bsz * seq, A_HEADS * A_DH), BF16),
        scratch_shapes=[
            pltpu.VMEM((seq, Q_BLOCK), F32),
            pltpu.VMEM((Q_BLOCK, seq), F32),
            pltpu.VMEM((A_KV, rows, LANE), F32),
            pltpu.VMEM((A_KV, rows, 2 * A_DH), F32),
        ],
        compiler_params=_cparams(("parallel", "arbitrary")),
        name="dsa_prompt",
    )(iq, iw, ik3, aq, ak3, av3)


IDX_PAGES = 16
ATT_PAGES = 8
PAGE_ROWS = PAGE * A_KV


def _dsa_sample_index_kernel(pt_ref, *refs, n_pages, topk):
    page_refs = refs[:IDX_PAGES]
    iqt_ref, w_ref, iknew_ref, sel_ref, selnew_ref, sc_ref = refs[IDX_PAGES:]
    g = pl.program_id(1)

    @pl.when(g == 0)
    def _():
        sc_ref[...] = jnp.zeros_like(sc_ref)

    iqt = iqt_ref[0]
    w = w_ref[0]
    lane = lax.broadcasted_iota(I32, (PAGE, LANE), 1)

    def col_score(keys_f32):
        s = jnp.dot(keys_f32.astype(BF16), iqt, preferred_element_type=F32)
        return jnp.sum(jnp.maximum(s, 0.0) * w, axis=1, keepdims=True)

    acc = sc_ref[...]
    for j in range(IDX_PAGES):
        col = col_score(page_refs[j][0])
        acc = acc + jnp.where(lane == g * IDX_PAGES + j, col, 0.0)
    sc_ref[...] = acc

    @pl.when(g == n_pages // IDX_PAGES - 1)
    def _():
        past = sc_ref[...]
        row = lax.broadcasted_iota(I32, (PAGE, 1), 0)
        new = jnp.where(row == 0, col_score(iknew_ref[0]), -jnp.inf)
        newk = new

        def count_ge(cand):
            hits = jnp.sum(jnp.where(past >= cand, 1, 0).astype(I32), axis=0, keepdims=True)
            hits = jnp.sum(hits, axis=1, keepdims=True)
            return hits + jnp.sum(jnp.where(newk >= cand, 1, 0).astype(I32), axis=0, keepdims=True)

        thr = _kth_largest(count_ge, topk, (1, 1))
        sel = jnp.where(past >= thr, 1.0, 0.0).astype(BF16)
        r = lax.broadcasted_iota(I32, (PAGE_ROWS, PAGE), 0)
        k = lax.broadcasted_iota(I32, (PAGE_ROWS, PAGE), 1)
        expand = jnp.where(r // A_KV == k, 1.0, 0.0).astype(BF16)
        sel_ref[0] = jnp.dot(expand, sel, preferred_element_type=F32).astype(sel_ref.dtype)
        new_sel = jnp.where(newk[0:1, :] >= thr, 1.0, 0.0)
        rr = lax.broadcasted_iota(I32, (PAGE_ROWS, LANE), 0)
        selnew_ref[0] = jnp.where(rr < A_KV, new_sel, 0.0)


def _dsa_sample_attn_kernel(pt_ref, *refs, n_pages):
    k_refs = refs[:ATT_PAGES]
    v_refs = refs[ATT_PAGES:2 * ATT_PAGES]
    qt_ref, sel_ref, selnew_ref, knew_ref, vnew_ref, o_ref, m_ref, l_ref, acc_ref = refs[2 * ATT_PAGES:]
    g = pl.program_id(1)

    @pl.when(g == 0)
    def _():
        m_ref[...] = jnp.full(m_ref.shape, NEG_BIG, F32)
        l_ref[...] = jnp.zeros_like(l_ref)
        acc_ref[...] = jnp.zeros_like(acc_ref)

    qt = qt_ref[0]
    r = lax.broadcasted_iota(I32, (PAGE_ROWS, LANE), 0)
    lane = lax.broadcasted_iota(I32, (PAGE_ROWS, LANE), 1)
    head_bias = jnp.where(r % A_KV == lane // A_GROUP, 0.0, NEG_BIG)
    pick_r = lax.broadcasted_iota(I32, (PAGE, LANE), 0)
    tn = (((0,), (0,)), ((), ()))

    def step(pages):
        s = [jnp.dot(k.astype(BF16), qt, preferred_element_type=F32) + head_bias + (sel_b - 1.0) * (-NEG_BIG)
             for k, _, sel_b in pages]
        m_old = m_ref[...]
        m_new = m_old
        for sc in s:
            m_new = jnp.maximum(m_new, jnp.max(sc, axis=0, keepdims=True))
        alpha = jnp.exp2(m_old - m_new)
        l = alpha * l_ref[...]
        acc = alpha * acc_ref[...]
        for sc, (_, v, _) in zip(s, pages):
            p = jnp.exp2(sc - m_new)
            l = l + jnp.sum(p, axis=0, keepdims=True)
            acc = acc + lax.dot_general(v.astype(BF16), p.astype(BF16), tn, preferred_element_type=F32)
        l_ref[...] = l
        acc_ref[...] = acc
        m_ref[...] = m_new

    sel = sel_ref[0]
    pages = []
    for j in range(ATT_PAGES):
        onehot = jnp.where(pick_r == g * ATT_PAGES + j, 1.0, 0.0).astype(BF16)
        sel_b = jnp.dot(sel, onehot, preferred_element_type=F32)
        pages.append((k_refs[j][0], v_refs[j][0], sel_b))
    step(pages)

    @pl.when(g == n_pages // ATT_PAGES - 1)
    def _():
        step([(knew_ref[0], vnew_ref[0], selnew_ref[0])])
        o_ref[0] = acc_ref[...] / l_ref[...]


def _dsa_sample(aq, ak, av, iq, iw, ik, pool_k, pool_v, pool_kidx, page_table):
    bd, n_pages = page_table.shape
    n_pool = pool_k.shape[0]
    topk = min(TOPK_MAX, (n_pages * PAGE + 1) // 4)
    assert n_pages == LANE and n_pages % IDX_PAGES == 0 and n_pages % ATT_PAGES == 0
    iqt = jnp.zeros((bd, I_DIM, LANE), BF16).at[:, :, :I_HEADS].set(
        iq.reshape(bd, I_HEADS, I_DIM).transpose(0, 2, 1))
    w_row = jnp.where(jnp.arange(LANE) < I_HEADS, iw, 0.0).reshape(bd, 1, LANE)
    ik_new = jnp.zeros((bd, PAGE, I_DIM), F32).at[:, 0, :].set(ik)
    new_rows = lambda a: jnp.zeros((bd, PAGE_ROWS, A_DH), F32).at[:, :A_KV, :].set(a)
    k_new, v_new = new_rows(ak), new_rows(av)
    qt = jnp.zeros((bd, A_DH, LANE), BF16).at[:, :, :A_HEADS].set(
        aq.reshape(bd, A_HEADS, A_DH).transpose(0, 2, 1))

    def page_spec(rows, width, j, per):
        return pl.BlockSpec((1, rows, width), lambda b, g, pt: (pt[b, g * per + j], 0, 0))

    per_row = lambda shape: pl.BlockSpec((1,) + shape, lambda b, g, pt: (b, 0, 0))
    sel, sel_new = pl.pallas_call(
        functools.partial(_dsa_sample_index_kernel, n_pages=n_pages, topk=topk),
        grid_spec=pltpu.PrefetchScalarGridSpec(
            num_scalar_prefetch=1,
            grid=(bd, n_pages // IDX_PAGES),
            in_specs=[page_spec(PAGE, I_DIM, j, IDX_PAGES) for j in range(IDX_PAGES)]
            + [per_row((I_DIM, LANE)), per_row((1, LANE)), per_row((PAGE, I_DIM))],
            out_specs=[per_row((PAGE_ROWS, PAGE)), per_row((PAGE_ROWS, LANE))],
            scratch_shapes=[pltpu.VMEM((PAGE, LANE), F32)],
        ),
        out_shape=[jax.ShapeDtypeStruct((bd, PAGE_ROWS, PAGE), BF16),
                   jax.ShapeDtypeStruct((bd, PAGE_ROWS, LANE), F32)],
        compiler_params=_cparams(("parallel", "arbitrary")),
        name="dsa_sample_index",
    )(page_table, *([pool_kidx] * IDX_PAGES), iqt, w_row, ik_new)

    pk = pool_k.reshape(n_pool, PAGE_ROWS, A_DH)
    pv = pool_v.reshape(n_pool, PAGE_ROWS, A_DH)
    o_t = pl.pallas_call(
        functools.partial(_dsa_sample_attn_kernel, n_pages=n_pages),
        grid_spec=pltpu.PrefetchScalarGridSpec(
            num_scalar_prefetch=1,
            grid=(bd, n_pages // ATT_PAGES),
            in_specs=[page_spec(PAGE_ROWS, A_DH, j, ATT_PAGES) for j in range(ATT_PAGES)] * 2
            + [per_row((A_DH, LANE)), per_row((PAGE_ROWS, PAGE)), per_row((PAGE_ROWS, LANE)),
               per_row((PAGE_ROWS, A_DH)), per_row((PAGE_ROWS, A_DH))],
            out_specs=per_row((A_DH, LANE)),
            scratch_shapes=[pltpu.VMEM((1, LANE), F32), pltpu.VMEM((1, LANE), F32), pltpu.VMEM((A_DH, LANE), F32)],
        ),
        out_shape=jax.ShapeDtypeStruct((bd, A_DH, LANE), F32),
        compiler_params=_cparams(("parallel", "arbitrary")),
        name="dsa_sample_attn",
    )(page_table, *([pk] * ATT_PAGES), *([pv] * ATT_PAGES), qt, sel, sel_new, k_new, v_new)
    return o_t[:, :, :A_HEADS].transpose(0, 2, 1).reshape(bd, A_HEADS * A_DH).astype(BF16)


def _gate_mix_kernel(rn_ref, att_ref, wr_ref, wa_ref, gr_ref, ga_ref, o_ref):
    r = jnp.dot(rn_ref[...], wr_ref[...], preferred_element_type=F32)
    a = jnp.dot(att_ref[...], wa_ref[...], preferred_element_type=F32)
    o_ref[...] = (gr_ref[...].astype(F32) * r + ga_ref[...].astype(F32) * a).astype(o_ref.dtype)


def _out_proj_kernel(x_ref, mix_ref, wo_ref, nw_ref, h_ref, hn_ref):
    h = x_ref[...] + jnp.dot(mix_ref[...], wo_ref[...], preferred_element_type=F32)
    h_ref[...] = h
    ms = jnp.mean(h * h, axis=-1, keepdims=True)
    hn_ref[...] = (h * lax.rsqrt(ms + EPS) * nw_ref[...]).astype(hn_ref.dtype)


def _mixer_out(x2d, rn, att, gates, wr, wa, wo, norm2_w, tm):
    n, d = x2d.shape
    tn = 512
    ncol = d // tn
    mixed = pl.pallas_call(
        _gate_mix_kernel,
        grid=(ncol, n // tm),
        in_specs=[
            pl.BlockSpec((tm, rn.shape[1]), lambda j, i: (i, 0)),
            pl.BlockSpec((tm, att.shape[1]), lambda j, i: (i, 0)),
            pl.BlockSpec((wr.shape[0], tn), lambda j, i: (0, j)),
            pl.BlockSpec((wa.shape[0], tn), lambda j, i: (0, j)),
            pl.BlockSpec((tm, tn), lambda j, i: (i, j)),
            pl.BlockSpec((tm, tn), lambda j, i: (i, ncol + j)),
        ],
        out_specs=pl.BlockSpec((tm, tn), lambda j, i: (i, j)),
        out_shape=jax.ShapeDtypeStruct((n, d), BF16),
        compiler_params=_cparams(("parallel", "parallel")),
        name="gate_mix",
    )(rn, att, wr, wa, gates, gates)
    tm2 = min(tm, 256)
    h, hn = pl.pallas_call(
        _out_proj_kernel,
        grid=(n // tm2,),
        in_specs=[
            pl.BlockSpec((tm2, d), lambda i: (i, 0)),
            pl.BlockSpec((tm2, d), lambda i: (i, 0)),
            pl.BlockSpec((d, d), lambda i: (0, 0)),
            pl.BlockSpec((1, d), lambda i: (0, 0)),
        ],
        out_specs=[pl.BlockSpec((tm2, d), lambda i: (i, 0)), pl.BlockSpec((tm2, d), lambda i: (i, 0))],
        out_shape=[jax.ShapeDtypeStruct((n, d), F32), jax.ShapeDtypeStruct((n, d), BF16)],
        compiler_params=_cparams(("parallel",)),
        name="out_proj",
    )(x2d, mixed, wo, norm2_w.reshape(1, d))
    return h, hn


PEER_TE = 1024
NO_RANK = 127.0


def _top_desc(x, count):
    vals = []
    cur = x
    for _ in range(count):
        mx = jnp.max(cur, axis=0, keepdims=True)
        vals.append(mx)
        cur = jnp.where(cur == mx, -jnp.inf, cur)
    return vals


def _top_desc_rank(x, count):
    vals = []
    cur = x
    rank = jnp.full(x.shape, NO_RANK, F32)
    for i in range(count):
        mx = jnp.max(cur, axis=0, keepdims=True)
        hit = cur == mx
        vals.append(mx)
        rank = jnp.where(hit, float(i), rank)
        cur = jnp.where(hit, -jnp.inf, cur)
    return vals, rank


def _peer_route_kernel(q_ref, k1_ref, k2_ref, n1_ref, e1_ref, r2_ref, e2_ref):
    nt = (((1,), (1,)), ((), ()))
    half = P_TOPK // 2
    for h in range(P_HEADS):
        qh = q_ref[:, h * P_DKEY:(h + 1) * P_DKEY]
        s1 = lax.dot_general(k1_ref[h], qh, nt, preferred_element_type=F32)
        s2 = lax.dot_general(k2_ref[h], qh, nt, preferred_element_type=F32)
        v1, r1 = _top_desc_rank(s1, P_TOPK)
        v2, r2 = _top_desc_rank(s2, P_TOPK)
        v1blk = jnp.concatenate(v1, axis=0)
        v2blk = jnp.concatenate(v2, axis=0)
        blocks = ([v1[0] + v2blk] + [v1[i] + v2blk[:half] for i in range(1, half)]
                  + [v1blk[half:] + v2[0]])
        c = _top_desc(jnp.concatenate(blocks, axis=0), P_TOPK)
        thr = c[-1]
        z = jnp.zeros_like(thr)
        for i in range(P_TOPK):
            z = z + jnp.exp(c[i] - c[0])
        counts = [jnp.sum(jnp.where(blk >= thr, 1.0, 0.0), axis=0, keepdims=True) for blk in blocks[:half]]
        tail = jnp.where(blocks[half] >= thr, 1.0, 0.0)
        counts += [tail[i:i + 1] for i in range(half)]
        n1 = jnp.zeros_like(s1)
        for i in range(P_TOPK):
            n1 = jnp.where(r1 == float(i), counts[i], n1)
        n1_ref[h] = n1
        e1_ref[h] = jnp.where(r1 < P_TOPK, jnp.exp(s1 - v1[0]) / z, 0.0)
        r2_ref[h] = r2.astype(r2_ref.dtype)
        e2_ref[h] = jnp.where(r2 < P_TOPK, jnp.exp(s2 - v2[0]), 0.0).astype(e2_ref.dtype)


def _peer_expert_kernel(hn_ref, h_ref, u_ref, vt_ref, n1_ref, e1_ref, r2_ref, e2_ref,
                        y_ref, acc_ref, w_ref, *, a_per_tile):
    j = pl.program_id(1)

    @pl.when(j == 0)
    def _():
        acc_ref[...] = jnp.zeros_like(acc_ref)

    tm = w_ref.shape[1]
    for al in range(a_per_tile):
        a = j * a_per_tile + al
        w = jnp.zeros((P_NKEYS, tm), BF16)
        for h in range(P_HEADS):
            nrow = jnp.broadcast_to(n1_ref[h, pl.ds(a, 1), :].astype(BF16), (P_NKEYS, tm))
            erow = jnp.broadcast_to(e1_ref[h, pl.ds(a, 1), :].astype(BF16), (P_NKEYS, tm))
            w = w + jnp.where(r2_ref[h] < nrow, e2_ref[h], jnp.zeros((), BF16)) * erow
        w_ref[al * P_NKEYS:(al + 1) * P_NKEYS, :] = w

    nt = (((1,), (1,)), ((), ()))
    x = lax.dot_general(u_ref[...], hn_ref[...], nt, preferred_element_type=F32)
    p = w_ref[...] * (0.5 * x * (1.0 + lax.erf(x * (2.0 ** -0.5)))).astype(BF16)
    acc_ref[...] += jnp.dot(vt_ref[...], p, preferred_element_type=F32)

    @pl.when(j == pl.num_programs(1) - 1)
    def _():
        y_ref[...] = h_ref[...] + acc_ref[...].T


def _peer(hn, h, wq_t, k1p, k2p, u_bf, vt_bf, tm):
    n, d = hn.shape
    te = PEER_TE
    hk = P_HEADS * P_DKEY
    tr = min(tm, 128)
    tok = lambda i: (0, 0, i)
    head_blk = pl.BlockSpec((P_HEADS, P_NKEYS, tr), tok)
    unused = jnp.zeros((tm, LANE), F32)
    q, = _proj(hn, wq_t, (0, hk), unused, unused, mode="plain", out_dtypes=(BF16,), tm=tm, tn=512)
    n1, e1, r2, e2 = pl.pallas_call(
        _peer_route_kernel,
        grid=(n // tr,),
        in_specs=[
            pl.BlockSpec((tr, hk), lambda i: (i, 0)),
            pl.BlockSpec((P_HEADS, P_NKEYS, P_DKEY), lambda i: (0, 0, 0)),
            pl.BlockSpec((P_HEADS, P_NKEYS, P_DKEY), lambda i: (0, 0, 0)),
        ],
        out_specs=[head_blk] * 4,
        out_shape=[jax.ShapeDtypeStruct((P_HEADS, P_NKEYS, n), dt) for dt in (F32, F32, BF16, BF16)],
        compiler_params=_cparams(("parallel",)),
        name="peer_route",
    )(q, k1p, k2p)

    n_exp = u_bf.shape[0]
    once = pl.Buffered(1)
    tok2 = lambda i, j: (0, 0, i)
    head_f32 = pl.BlockSpec((P_HEADS, P_NKEYS, tm), tok2, pipeline_mode=once)
    head_bf16 = pl.BlockSpec((P_HEADS, P_NKEYS, tm), tok2)
    return pl.pallas_call(
        functools.partial(_peer_expert_kernel, a_per_tile=te // P_NKEYS),
        grid=(n // tm, n_exp // te),
        in_specs=[
            pl.BlockSpec((tm, d), lambda i, j: (i, 0)),
            pl.BlockSpec((tm, d), lambda i, j: (i, 0), pipeline_mode=once),
            pl.BlockSpec((te, d), lambda i, j: (j, 0)),
            pl.BlockSpec((d, te), lambda i, j: (0, j)),
            head_f32, head_f32, head_bf16, head_bf16,
        ],
        out_specs=pl.BlockSpec((tm, d), lambda i, j: (i, 0)),
        out_shape=jax.ShapeDtypeStruct((n, d), F32),
        scratch_shapes=[pltpu.VMEM((d, tm), F32), pltpu.VMEM((te, tm), BF16)],
        compiler_params=_cparams(("parallel", "arbitrary")),
        name="peer_experts",
    )(hn, h, u_bf, vt_bf, n1, e1, r2, e2)


def _prep_weights(w_in, w_ret_o, w_att_o, w_out, peer_wq, peer_keys1, peer_keys2, peer_u, peer_v):
    half = P_DKEY // 2
    k1p = jnp.pad(peer_keys1, ((0, 0), (0, 0), (0, half))).astype(BF16)
    k2p = jnp.pad(peer_keys2, ((0, 0), (0, 0), (half, 0))).astype(BF16)
    return dict(w_t=w_in.T,
                wr=w_ret_o.astype(BF16), wa=w_att_o.astype(BF16), wo=w_out.astype(BF16),
                wq_t=peer_wq.T, k1p=k1p, k2p=k2p,
                u_bf=peer_u.astype(BF16), vt_bf=peer_v.T.astype(BF16))


def _prompt_layer(x, W, norm1_w, q_norm_w, k_norm_w, ret_gn_w, norm2_w):
    bsz, seq, d = x.shape
    x2d = x.reshape(bsz * seq, d)
    tm = 512
    p = _mixer_in(x2d, jnp.arange(seq), norm1_w, W["w_t"], q_norm_w, k_norm_w, 2 * tm)
    s0 = jnp.zeros((bsz, R_HEADS, R_DK, R_DV), F32)
    rn, S = _retention_prompt(p["rq"], p["rk"], p["rvg"], ret_gn_w, s0, bsz, seq)
    att = _dsa_prompt(p["iq"], p["iw"], p["ik_bf"], p["aq"], p["ak_bf"], p["av_bf"], bsz, seq)
    h, hn = _mixer_out(x2d, rn, att, p["gates"], W["wr"], W["wa"], W["wo"], norm2_w, tm)
    y = _peer(hn, h, W["wq_t"], W["k1p"], W["k2p"], W["u_bf"], W["vt_bf"], tm=512)
    return (y.reshape(bsz, seq, d), p["ak"].reshape(bsz, seq, A_KV, A_DH), p["av"].reshape(bsz, seq, A_KV, A_DH),
            p["ik"].reshape(bsz, seq, I_DIM), S)


def _sample_layer(x, pool_k, pool_v, pool_kidx, s_in, page_table, W, norm1_w, q_norm_w, k_norm_w,
                  ret_gn_w, norm2_w):
    bd, ts, d = x.shape
    assert ts == 1
    x2d = x.reshape(bd, d)
    past = page_table.shape[1] * PAGE
    p = _mixer_in(x2d, jnp.full((bd,), past, I32), norm1_w, W["w_t"], q_norm_w, k_norm_w, bd)
    rn, S = _retention_sample(p["rq"], p["rk"], p["rvg"], ret_gn_w, s_in)
    att = _dsa_sample(p["aq"], p["ak"], p["av"], p["iq"], p["iw"], p["ik"], pool_k, pool_v, pool_kidx, page_table)
    h, hn = _mixer_out(x2d, rn, att, p["gates"], W["wr"], W["wa"], W["wo"], norm2_w, bd)
    pad = LANE - bd
    hn_p = jnp.pad(hn, ((0, pad), (0, 0)))
    h_p = jnp.pad(h, ((0, pad), (0, 0)))
    y = _peer(hn_p, h_p, W["wq_t"], W["k1p"], W["k2p"], W["u_bf"], W["vt_bf"], tm=LANE)[:bd]
    return (y.reshape(bd, ts, d), p["ak"].reshape(bd, ts, A_KV, A_DH), p["av"].reshape(bd, ts, A_KV, A_DH),
            p["ik"].reshape(bd, ts, I_DIM), S)


def kernel(x_prompt, x_sample, cache_k, cache_v, cache_k_idx, state_ret, page_table, norm1_w, w_in, q_norm_w,
           k_norm_w, ret_gn_w, w_ret_o, w_att_o, w_out, norm2_w, peer_wq, peer_keys1, peer_keys2, peer_u, peer_v):
    depth = w_in.shape[0]
    assert depth == 1
    l = 0
    W = _prep_weights(w_in[l], w_ret_o[l], w_att_o[l], w_out[l], peer_wq[l], peer_keys1[l], peer_keys2[l],
                      peer_u[l], peer_v[l])
    yp, kp, vp, ikp, sp = _prompt_layer(x_prompt, W, norm1_w[l], q_norm_w[l], k_norm_w[l], ret_gn_w[l], norm2_w[l])
    ys, kn, vn, ikn, sn = _sample_layer(x_sample, cache_k[l], cache_v[l], cache_k_idx[l], state_ret[l], page_table,
                                        W, norm1_w[l], q_norm_w[l], k_norm_w[l], ret_gn_w[l], norm2_w[l])
    return (yp, ys, kp[None], vp[None], ikp[None], sp[None], kn[None], vn[None], ikn[None], sn[None])
```

```python
import functools

import jax
import jax.numpy as jnp
from jax import lax
from jax.experimental import pallas as pl
from jax.experimental.pallas import tpu as pltpu

F32 = jnp.float32
BF16 = jnp.bfloat16
I32 = jnp.int32

D_MODEL = 2048
PAGE = 128
R_HEADS, R_DK, R_DV, R_CHUNK = 8, 128, 256, 128
A_HEADS, A_KV, A_DH = 16, 4, 128
A_GROUP = A_HEADS // A_KV
I_HEADS, I_DIM = 16, 128
TOPK_MAX = 256
Q_BLOCK = 128
ROPE_THETA = 10000.0
P_HEADS, P_NKEYS, P_DKEY, P_TOPK = 8, 128, 128, 16
EPS = 1e-6

LANE = 128
VMEM_LIMIT = 56 * 1024 * 1024
NEG_BIG = -1e30
LOG2E = 1.4426950408889634
INT_MIN = -2147483648
NEGINF_KEY = INT_MIN + 0x7FFFFF

SEG = dict(rq=(0, 1024), rk=(1024, 1024), rvg=(2048, 4096), aq=(6144, 2048), ak=(8192, 512), av=(8704, 512),
           iq=(9216, 2048), ik=(11264, 128), iw=(11392, 128))
GATE_COL = 11408
GATE_WIDTH = 2 * D_MODEL


def _cparams(sem):
    return pltpu.CompilerParams(dimension_semantics=sem, vmem_limit_bytes=VMEM_LIMIT)


def _rmsnorm_kernel(x_ref, w_ref, o_ref):
    x = x_ref[...]
    ms = jnp.mean(x * x, axis=-1, keepdims=True)
    o_ref[...] = (x * lax.rsqrt(ms + EPS) * w_ref[...]).astype(o_ref.dtype)


def _rmsnorm(x, w, tm):
    n, d = x.shape
    return pl.pallas_call(
        _rmsnorm_kernel,
        grid=(n // tm,),
        in_specs=[pl.BlockSpec((tm, d), lambda i: (i, 0)), pl.BlockSpec((1, d), lambda i: (0, 0))],
        out_specs=pl.BlockSpec((tm, d), lambda i: (i, 0)),
        out_shape=jax.ShapeDtypeStruct((n, d), BF16),
        compiler_params=_cparams(("parallel",)),
        name="rmsnorm",
    )(x, w.reshape(1, d))


def _proj_kernel(x_ref, w_ref, a_ref, b_ref, *refs, mode, scale, heads):
    out_refs, wb_ref = refs[:-1], refs[-1]

    @pl.when(pl.program_id(1) == 0)
    def _():
        wb_ref[...] = w_ref[...].astype(wb_ref.dtype)

    z = lax.dot_general(x_ref[...], wb_ref[...], (((1,), (1,)), ((), ())), preferred_element_type=F32)
    if mode in ("rope", "norm_rope"):
        a, b = a_ref[...], b_ref[...]
        parts = []
        for h in range(heads):
            zh = z[:, h * LANE:(h + 1) * LANE]
            y = zh * a + pltpu.roll(zh, LANE // 2, 1) * b
            if mode == "norm_rope":
                sq = zh * zh
                hi = sq.astype(BF16)
                lo = (sq - hi.astype(F32)).astype(BF16)
                avg = jnp.full((LANE, LANE), 1.0 / LANE, BF16)
                ms = (jnp.dot(hi, avg, preferred_element_type=F32) + jnp.dot(lo, avg, preferred_element_type=F32))
                y = y * lax.rsqrt(ms + EPS)
            parts.append(y)
        z = jnp.concatenate(parts, axis=1) if heads > 1 else parts[0]
    elif mode == "sigmoid":
        z = jax.nn.sigmoid(z)
    elif scale != 1.0:
        z = z * scale
    for o in out_refs:
        if len(o.shape) == 3:
            for h in range(heads):
                o[:, h, :] = z[:, h * LANE:(h + 1) * LANE].astype(o.dtype)
        else:
            o[...] = z.astype(o.dtype)


def _proj(xn, w_t, seg, a_tab, b_tab, *, mode, scale=1.0, out_dtypes, tm, tn, split_first=False):
    n, d = xn.shape
    col0, width = seg
    tn = min(tn, width)
    assert col0 % 8 == 0 and width % tn == 0 and n % tm == 0 and not (split_first and tn != width)
    pos_blocks = a_tab.shape[0] // tm
    out_blocks = [pl.BlockSpec((tm, tn), lambda j, i: (i, j)) for _ in out_dtypes]
    out_shapes = [jax.ShapeDtypeStruct((n, width), dt) for dt in out_dtypes]
    if split_first:
        out_blocks[0] = pl.BlockSpec((tm, width // LANE, LANE), lambda j, i: (i, 0, 0))
        out_shapes[0] = jax.ShapeDtypeStruct((n, width // LANE, LANE), out_dtypes[0])
    kern = functools.partial(_proj_kernel, mode=mode, scale=scale, heads=tn // LANE)
    outs = pl.pallas_call(
        kern,
        grid=(width // tn, n // tm),
        in_specs=[
            pl.BlockSpec((tm, d), lambda j, i: (i, 0)),
            pl.BlockSpec((pl.Element(tn), pl.Element(d)), lambda j, i: (pl.multiple_of(col0 + j * tn, 8), 0)),
            pl.BlockSpec((tm, LANE), lambda j, i: (i % pos_blocks, 0)),
            pl.BlockSpec((tm, LANE), lambda j, i: (i % pos_blocks, 0)),
        ],
        out_specs=out_blocks,
        out_shape=out_shapes,
        scratch_shapes=[pltpu.VMEM((tn, d), BF16)],
        compiler_params=_cparams(("parallel", "arbitrary")),
        name="proj_" + mode,
    )(xn, w_t, a_tab, b_tab)
    return outs


def _rope_tables(pos):
    half = LANE // 2
    freqs = ROPE_THETA ** (-jnp.arange(half, dtype=F32) / half)
    ang = pos.astype(F32)[:, None] * freqs[None, :]
    cos, sin = jnp.cos(ang), jnp.sin(ang)
    return jnp.concatenate([cos, cos], axis=1), jnp.concatenate([-sin, sin], axis=1)


def _mixer_in(x2d, pos_rows, norm1_w, w_t, q_norm_w, k_norm_w, tm):
    xn = _rmsnorm(x2d, norm1_w, tm)
    cos2, sin2 = _rope_tables(pos_rows)
    swap = lambda g: jnp.roll(g, LANE // 2)
    rope = lambda gain, scale: (cos2 * (gain * scale)[None, :], sin2 * (swap(gain) * scale)[None, :])
    ones = jnp.ones((LANE,), F32)
    p = functools.partial(_proj, xn, tm=tm, tn=512)
    out = {}
    out["rq"], = p(w_t, SEG["rq"], *rope(ones, 1.0), mode="rope", out_dtypes=(BF16,))
    out["rk"], = p(w_t, SEG["rk"], *rope(ones, R_DK ** -0.5), mode="rope", out_dtypes=(BF16,))
    out["rvg"], = p(w_t, SEG["rvg"], cos2, sin2, mode="plain", out_dtypes=(BF16,))
    out["aq"], = p(w_t, SEG["aq"], *rope(q_norm_w, A_DH ** -0.5 * LOG2E), mode="norm_rope", out_dtypes=(BF16,))
    out["ak"], out["ak_bf"] = p(w_t, SEG["ak"], *rope(k_norm_w, 1.0), mode="norm_rope", out_dtypes=(F32, BF16),
                                 split_first=True)
    out["av"], out["av_bf"] = p(w_t, SEG["av"], cos2, sin2, mode="plain", out_dtypes=(F32, BF16), split_first=True)
    out["iq"], = p(w_t, SEG["iq"], *rope(ones, I_DIM ** -0.5), mode="rope", out_dtypes=(BF16,))
    out["ik"], out["ik_bf"] = p(w_t, SEG["ik"], *rope(ones, 1.0), mode="rope", out_dtypes=(F32, BF16))
    out["iw"], = p(w_t, SEG["iw"], cos2, sin2, mode="plain", scale=I_HEADS ** -0.5, out_dtypes=(F32,))
    out["gates"], = p(w_t, (GATE_COL, GATE_WIDTH), cos2, sin2, mode="sigmoid", out_dtypes=(BF16,))
    return out


def _ret_tables(chunk):
    log_g = jnp.log1p(-jnp.exp2(-5.0 - jnp.arange(R_HEADS, dtype=F32)))
    n = jnp.arange(chunk, dtype=F32)
    diff = n[:, None] - n[None, :]
    causal = diff >= 0
    dmat = jnp.where(causal, jnp.exp(jnp.where(causal, diff, 0.0)[None] * log_g[:, None, None]), 0.0)
    qdec = jnp.exp((n + 1.0)[None, :] * log_g[:, None])
    kdec = jnp.exp((chunk - 1.0 - n)[None, :] * log_g[:, None])
    gc = jnp.exp(chunk * log_g)
    return dmat, qdec, kdec, gc


def _ret_kernel(q_ref, k_ref, v_ref, g_ref, gnw_ref, dmat_ref, qdec_ref, kdec_ref, gc_ref, s0_ref,
                o_ref, s_ref):
    c = pl.program_id(1)

    @pl.when(c == 0)
    def _():
        s_ref[...] = s0_ref[...]

    nt = (((1,), (1,)), ((), ()))
    tn = (((0,), (0,)), ((), ()))
    for h in range(R_HEADS):
        S = s_ref[0, h]
        q = q_ref[:, h * R_DK:(h + 1) * R_DK]
        k = k_ref[:, h * R_DK:(h + 1) * R_DK]
        v = v_ref[:, h * R_DV:(h + 1) * R_DV]
        scores = lax.dot_general(q, k, nt, preferred_element_type=F32) * dmat_ref[h]
        intra = jnp.dot(scores.astype(BF16), v, preferred_element_type=F32)
        qd = (q.astype(F32) * qdec_ref[h]).astype(BF16)
        cross = jnp.dot(qd, S.astype(BF16), preferred_element_type=F32)
        kd = (k.astype(F32) * kdec_ref[h]).astype(BF16)
        s_ref[0, h] = gc_ref[h] * S + lax.dot_general(kd, v, tn, preferred_element_type=F32)
        o = intra + cross
        mu = jnp.mean(o, axis=-1, keepdims=True)
        var = jnp.mean(jnp.square(o - mu), axis=-1, keepdims=True)
        g = g_ref[:, h * R_DV:(h + 1) * R_DV].astype(F32)
        rn = (o - mu) * lax.rsqrt(var + EPS) * gnw_ref[:, h * R_DV:(h + 1) * R_DV] * (g * jax.nn.sigmoid(g))
        o_ref[:, h * R_DV:(h + 1) * R_DV] = rn.astype(o_ref.dtype)


def _retention_prompt(rq, rk, rvg, gn_w, s0, bsz, seq):
    C = R_CHUNK
    nc = seq // C
    dmat, qdec, kdec, gc = _ret_tables(C)
    qdec = jnp.broadcast_to(qdec[:, :, None], (R_HEADS, C, R_DK))
    kdec = jnp.broadcast_to(kdec[:, :, None], (R_HEADS, C, R_DK))
    gc = jnp.broadcast_to(gc[:, None, None], (R_HEADS, 1, R_DV))
    hk, hv = R_HEADS * R_DK, R_HEADS * R_DV
    row = lambda b, c: (b * nc + c, 0)
    const3 = lambda b, c: (0, 0, 0)
    state = pl.BlockSpec((1, R_HEADS, R_DK, R_DV), lambda b, c: (b, 0, 0, 0))
    rn, S = pl.pallas_call(
        _ret_kernel,
        grid=(bsz, nc),
        in_specs=[
            pl.BlockSpec((C, hk), row),
            pl.BlockSpec((C, hk), row),
            pl.BlockSpec((C, hv), row),
            pl.BlockSpec((C, hv), lambda b, c: (b * nc + c, 1)),
            pl.BlockSpec((1, hv), lambda b, c: (0, 0)),
            pl.BlockSpec((R_HEADS, C, C), const3),
            pl.BlockSpec((R_HEADS, C, R_DK), const3),
            pl.BlockSpec((R_HEADS, C, R_DK), const3),
            pl.BlockSpec((R_HEADS, 1, R_DV), const3),
            state,
        ],
        out_specs=[pl.BlockSpec((C, hv), row), state],
        out_shape=[jax.ShapeDtypeStruct((bsz * seq, hv), BF16),
                   jax.ShapeDtypeStruct((bsz, R_HEADS, R_DK, R_DV), F32)],
        compiler_params=_cparams(("parallel", "arbitrary")),
        name="retention",
    )(rq, rk, rvg, rvg, gn_w.reshape(1, -1), dmat, qdec, kdec, gc, s0)
    return rn, S


def _ret_step_kernel(qc_ref, kc_ref, v_ref, g_ref, gnw_ref, gam_ref, s0_ref, o_ref, s_ref):
    for h in range(R_HEADS):
        S = s0_ref[0, h]
        qc, kc = qc_ref[0, h], kc_ref[0, h]
        v = v_ref[0, h]
        gam = gam_ref[h]
        qk = jnp.sum(qc * kc, axis=0, keepdims=True)
        cross = jnp.sum((qc * gam) * S, axis=0, keepdims=True)
        o = qk * v + cross
        s_ref[0, h] = gam * S + kc * v
        mu = jnp.mean(o, axis=-1, keepdims=True)
        var = jnp.mean(jnp.square(o - mu), axis=-1, keepdims=True)
        g = g_ref[0, h]
        o_ref[0, h] = (o - mu) * lax.rsqrt(var + EPS) * gnw_ref[0, h] * (g * jax.nn.sigmoid(g))


def _retention_sample(rq, rk, rvg, gn_w, s0):
    bd = rq.shape[0]
    _, qdec, _, _ = _ret_tables(1)
    gam = jnp.broadcast_to(qdec[:, :, None], (R_HEADS, 1, R_DV)).astype(F32)
    qc = rq.astype(F32).reshape(bd, R_HEADS, R_DK, 1)
    kc = rk.astype(F32).reshape(bd, R_HEADS, R_DK, 1)
    rv = rvg[:, :R_HEADS * R_DV].astype(F32).reshape(bd, R_HEADS, 1, R_DV)
    rg = rvg[:, R_HEADS * R_DV:].astype(F32).reshape(bd, R_HEADS, 1, R_DV)
    gnw = gn_w.reshape(1, R_HEADS, 1, R_DV)
    col = pl.BlockSpec((1, R_HEADS, R_DK, 1), lambda b: (b, 0, 0, 0))
    rowv = pl.BlockSpec((1, R_HEADS, 1, R_DV), lambda b: (b, 0, 0, 0))
    st = pl.BlockSpec((1, R_HEADS, R_DK, R_DV), lambda b: (b, 0, 0, 0))
    rn, S = pl.pallas_call(
        _ret_step_kernel,
        grid=(bd,),
        in_specs=[col, col, rowv, rowv,
                  pl.BlockSpec((1, R_HEADS, 1, R_DV), lambda b: (0, 0, 0, 0)),
                  pl.BlockSpec((R_HEADS, 1, R_DV), lambda b: (0, 0, 0)), st],
        out_specs=[rowv, st],
        out_shape=[jax.ShapeDtypeStruct((bd, R_HEADS, 1, R_DV), F32),
                   jax.ShapeDtypeStruct((bd, R_HEADS, R_DK, R_DV), F32)],
        compiler_params=_cparams(("parallel",)),
        name="retention_step",
    )(qc, kc, rv, rg, gnw, gam, s0)
    return rn.reshape(bd, R_HEADS * R_DV).astype(BF16), S


def _key_to_float(key):
    return pltpu.bitcast(key ^ ((key >> 31) & 0x7FFFFFFF), F32)


def _kth_largest(count_ge, k, shape):
    cur = jnp.where(count_ge(jnp.zeros(shape, F32)) >= k, 0, INT_MIN).astype(I32)

    def body(i, cur):
        cand = cur | jnp.left_shift(jnp.int32(1), 30 - i)
        return jnp.where(count_ge(_key_to_float(cand)) >= k, cand, cur)

    key = lax.fori_loop(0, 31, body, cur)
    return jnp.where(key <= NEGINF_KEY, jnp.finfo(F32).min, _key_to_float(jnp.maximum(key, NEGINF_KEY + 1)))


KEY_UNIT = 256
SEARCH_KEYS = 512
TRIP_KEYS = 1024


def _dsa_prompt_kernel(iq_ref, iw_ref, ik_ref, aq_ref, ak_ref, av_ref, o_ref,
                       keys_ref, bias_ref, m_ref, acc_ref, *, topk):
    qb = pl.program_id(1)
    ntrips = (qb * Q_BLOCK + Q_BLOCK + TRIP_KEYS - 1) // TRIP_KEYS
    nt = (((1,), (1,)), ((), ()))

    iq = iq_ref[...]
    pair = lambda hp: jnp.concatenate([iq[:, (2 * hp) * LANE:(2 * hp + 1) * LANE],
                                       iq[:, (2 * hp + 1) * LANE:(2 * hp + 2) * LANE]], axis=0)
    rhs = [pair(hp) for hp in range(I_HEADS // 2)]
    w_t = iw_ref[...].T
    t_idx = qb * Q_BLOCK + lax.broadcasted_iota(I32, (Q_BLOCK, Q_BLOCK), 1)
    row_iota = lax.broadcasted_iota(I32, (Q_BLOCK, Q_BLOCK), 0)

    def score_trip(t, carry):
        for c in range(TRIP_KEYS // Q_BLOCK):
            off = pl.multiple_of(t * TRIP_KEYS + c * Q_BLOCK, Q_BLOCK)
            ikc = ik_ref[0, pl.ds(off, Q_BLOCK), :]
            sc = jnp.zeros((Q_BLOCK, Q_BLOCK), F32)
            for hp in range(I_HEADS // 2):
                prod = lax.dot_general(ikc, rhs[hp], nt, preferred_element_type=F32)
                sc = sc + (jnp.maximum(prod[:, :Q_BLOCK], 0.0) * w_t[2 * hp:2 * hp + 1, :]
                           + jnp.maximum(prod[:, Q_BLOCK:], 0.0) * w_t[2 * hp + 1:2 * hp + 2, :])
            sc = jnp.where(off + row_iota <= t_idx, sc, -jnp.inf)
            keys_ref[pl.ds(off, Q_BLOCK), :] = sc
        return carry

    lax.fori_loop(0, ntrips, score_trip, 0)

    nsearch = (qb * Q_BLOCK + Q_BLOCK + SEARCH_KEYS - 1) // SEARCH_KEYS

    def count_ge(cand):
        def body(u, acc):
            for c in range(SEARCH_KEYS // KEY_UNIT):
                off = pl.multiple_of(u * SEARCH_KEYS + c * KEY_UNIT, KEY_UNIT)
                blk = keys_ref[pl.ds(off, KEY_UNIT), :]
                hit = jnp.where(blk >= cand, 1, 0).astype(I32)
                acc = acc + jnp.sum(hit.reshape(KEY_UNIT // 8, 8, Q_BLOCK), axis=0)
            return acc
        acc = lax.fori_loop(0, nsearch, body, jnp.zeros((8, Q_BLOCK), I32))
        return jnp.sum(acc, axis=0, keepdims=True)

    thr = _kth_largest(count_ge, topk, (1, Q_BLOCK))

    def bias_trip(t, carry):
        for c in range(TRIP_KEYS // Q_BLOCK):
            off = pl.multiple_of(t * TRIP_KEYS + c * Q_BLOCK, Q_BLOCK)
            sel = keys_ref[pl.ds(off, Q_BLOCK), :] >= thr
            bias_ref[:, pl.ds(off, Q_BLOCK)] = jnp.where(sel, 0.0, NEG_BIG).astype(F32).T
        return carry

    lax.fori_loop(0, ntrips, bias_trip, 0)

    rows = A_GROUP * Q_BLOCK
    ones_v = jnp.ones((KEY_UNIT, A_DH), BF16)
    subs = TRIP_KEYS // KEY_UNIT

    def logits(n, off):
        qg = jnp.concatenate([aq_ref[:, (n * A_GROUP + g) * A_DH:(n * A_GROUP + g + 1) * A_DH]
                              for g in range(A_GROUP)], axis=0)
        kch = ak_ref[0, pl.ds(off, KEY_UNIT), n * A_DH:(n + 1) * A_DH]
        b = bias_ref[:, pl.ds(off, KEY_UNIT)]
        s = lax.dot_general(qg, kch, nt, preferred_element_type=F32)
        return s + jnp.concatenate([b] * A_GROUP, axis=0)

    m_ref[...] = jnp.full((A_KV, rows, LANE), NEG_BIG, F32)
    acc_ref[...] = jnp.zeros((A_KV, rows, 2 * A_DH), F32)

    def attend(t, carry):
        for n in range(A_KV):
            offs = [pl.multiple_of(t * TRIP_KEYS + c * KEY_UNIT, KEY_UNIT) for c in range(subs)]
            s = [logits(n, off) for off in offs]
            m_old = m_ref[n]
            m_new = m_old
            for sc in s:
                m_new = jnp.maximum(m_new, jnp.max(sc, axis=-1, keepdims=True))
            alpha = jnp.exp2(m_old - m_new)
            m = jnp.concatenate([m_new] * (KEY_UNIT // LANE), axis=1)
            pv = jnp.zeros((rows, 2 * A_DH), F32)
            for sc, off in zip(s, offs):
                p = jnp.exp2(sc - m).astype(BF16)
                vch = av_ref[0, pl.ds(off, KEY_UNIT), n * A_DH:(n + 1) * A_DH]
                pv = pv + jnp.dot(p, jnp.concatenate([vch, ones_v], axis=1), preferred_element_type=F32)
            acc_ref[n] = acc_ref[n] * jnp.concatenate([alpha, alpha], axis=1) + pv
            m_ref[n] = m_new
        return carry

    lax.fori_loop(0, ntrips, attend, 0)
    for n in range(A_KV):
        acc = acc_ref[n]
        o = acc[:, :A_DH] / acc[:, A_DH:]
        for g in range(A_GROUP):
            hh = n * A_GROUP + g
            o_ref[:, hh * A_DH:(hh + 1) * A_DH] = o[g * Q_BLOCK:(g + 1) * Q_BLOCK].astype(o_ref.dtype)


def _dsa_prompt(iq, iw, ik_bf, aq, ak_bf, av_bf, bsz, seq):
    nb = seq // Q_BLOCK
    topk = min(TOPK_MAX, seq // 4)
    assert seq % TRIP_KEYS == 0
    ik3 = ik_bf.reshape(bsz, seq, I_DIM)
    ak3 = ak_bf.reshape(bsz, seq, A_KV * A_DH)
    av3 = av_bf.reshape(bsz, seq, A_KV * A_DH)
    rowblk = lambda b, q: (b * nb + q, 0)
    rows = A_GROUP * Q_BLOCK
    return pl.pallas_call(
        functools.partial(_dsa_prompt_kernel, topk=topk),
        grid=(bsz, nb),
        in_specs=[
            pl.BlockSpec((Q_BLOCK, I_HEADS * I_DIM), rowblk),
            pl.BlockSpec((Q_BLOCK, LANE), rowblk),
            pl.BlockSpec((1, seq, I_DIM), lambda b, q: (b, 0, 0)),
            pl.BlockSpec((Q_BLOCK, A_HEADS * A_DH), rowblk),
            pl.BlockSpec((1, seq, A_KV * A_DH), lambda b, q: (b, 0, 0)),
            pl.BlockSpec((1, seq, A_KV * A_DH), lambda b, q: (b, 0, 0)),
        ],
        out_specs=pl.BlockSpec((Q_BLOCK, A_HEADS * A_DH), rowblk),
        out_shape=jax.ShapeDtypeStruct((bsz * seq, A_HEADS * A_DH), BF16),
        scratch_shapes=[
            pltpu.VMEM((seq, Q_BLOCK), F32),
            pltpu.VMEM((Q_BLOCK, seq), F32),
            pltpu.VMEM((A_KV, rows, LANE), F32),
            pltpu.VMEM((A_KV, rows, 2 * A_DH), F32),
        ],
        compiler_params=_cparams(("parallel", "arbitrary")),
        name="dsa_prompt",
    )(iq, iw, ik3, aq, ak3, av3)


IDX_PAGES = 16
ATT_PAGES = 16
PAGE_ROWS = PAGE * A_KV


def _dsa_sample_index_kernel(pt_ref, *refs, n_pages, topk):
    page_refs = refs[:IDX_PAGES]
    iqt_ref, w_ref, iknew_ref, sel_ref, selnew_ref, sc_ref = refs[IDX_PAGES:]
    g = pl.program_id(1)

    @pl.when(g == 0)
    def _():
        sc_ref[...] = jnp.zeros_like(sc_ref)

    iqt = iqt_ref[0]
    w = w_ref[0]
    lane = lax.broadcasted_iota(I32, (PAGE, LANE), 1)

    def col_score(keys_f32):
        s = jnp.dot(keys_f32.astype(BF16), iqt, preferred_element_type=F32)
        return jnp.sum(jnp.maximum(s, 0.0) * w, axis=1, keepdims=True)

    acc = sc_ref[...]
    for j in range(IDX_PAGES):
        col = col_score(page_refs[j][0])
        acc = acc + jnp.where(lane == g * IDX_PAGES + j, col, 0.0)
    sc_ref[...] = acc

    @pl.when(g == n_pages // IDX_PAGES - 1)
    def _():
        past = sc_ref[...]
        row = lax.broadcasted_iota(I32, (PAGE, 1), 0)
        new = jnp.where(row == 0, col_score(iknew_ref[0]), -jnp.inf)
        newk = new

        def count_ge(cand):
            hits = jnp.sum(jnp.where(past >= cand, 1, 0).astype(I32), axis=0, keepdims=True)
            hits = jnp.sum(hits, axis=1, keepdims=True)
            return hits + jnp.sum(jnp.where(newk >= cand, 1, 0).astype(I32), axis=0, keepdims=True)

        thr = _kth_largest(count_ge, topk, (1, 1))
        sel = jnp.where(past >= thr, 1.0, 0.0).astype(BF16)
        r = lax.broadcasted_iota(I32, (PAGE_ROWS, PAGE), 0)
        k = lax.broadcasted_iota(I32, (PAGE_ROWS, PAGE), 1)
        expand = jnp.where(r // A_KV == k, 1.0, 0.0).astype(BF16)
        sel_ref[0] = jnp.dot(expand, sel, preferred_element_type=F32).astype(sel_ref.dtype)
        new_sel = jnp.where(newk[0:1, :] >= thr, 1.0, 0.0)
        rr = lax.broadcasted_iota(I32, (PAGE_ROWS, LANE), 0)
        selnew_ref[0] = jnp.where(rr < A_KV, new_sel, 0.0)


def _dsa_sample_attn_kernel(pt_ref, *refs, n_pages):
    k_refs = refs[:ATT_PAGES]
    v_refs = refs[ATT_PAGES:2 * ATT_PAGES]
    qt_ref, sel_ref, selnew_ref, knew_ref, vnew_ref, o_ref, m_ref, l_ref, acc_ref = refs[2 * ATT_PAGES:]
    g = pl.program_id(1)

    @pl.when(g == 0)
    def _():
        m_ref[...] = jnp.full(m_ref.shape, NEG_BIG, F32)
        l_ref[...] = jnp.zeros_like(l_ref)
        acc_ref[...] = jnp.zeros_like(acc_ref)

    qt = qt_ref[0]
    r = lax.broadcasted_iota(I32, (PAGE_ROWS, LANE), 0)
    lane = lax.broadcasted_iota(I32, (PAGE_ROWS, LANE), 1)
    head_bias = jnp.where(r % A_KV == lane // A_GROUP, 0.0, NEG_BIG)
    pick_r = lax.broadcasted_iota(I32, (PAGE, LANE), 0)
    tn = (((0,), (0,)), ((), ()))

    def step(pages):
        s = [jnp.dot(k.astype(BF16), qt, preferred_element_type=F32) + head_bias + (sel_b - 1.0) * (-NEG_BIG)
             for k, _, sel_b in pages]
        m_old = m_ref[...]
        m_new = m_old
        for sc in s:
            m_new = jnp.maximum(m_new, jnp.max(sc, axis=0, keepdims=True))
        alpha = jnp.exp2(m_old - m_new)
        l = alpha * l_ref[...]
        acc = alpha * acc_ref[...]
        for sc, (_, v, _) in zip(s, pages):
            p = jnp.exp2(sc - m_new)
            l = l + jnp.sum(p, axis=0, keepdims=True)
            acc = acc + lax.dot_general(v.astype(BF16), p.astype(BF16), tn, preferred_element_type=F32)
        l_ref[...] = l
        acc_ref[...] = acc
        m_ref[...] = m_new

    sel = sel_ref[0]
    pages = []
    for j in range(ATT_PAGES):
        onehot = jnp.where(pick_r == g * ATT_PAGES + j, 1.0, 0.0).astype(BF16)
        sel_b = jnp.dot(sel, onehot, preferred_element_type=F32)
        pages.append((k_refs[j][0], v_refs[j][0], sel_b))
    step(pages)

    @pl.when(g == n_pages // ATT_PAGES - 1)
    def _():
        step([(knew_ref[0], vnew_ref[0], selnew_ref[0])])
        o_ref[0] = acc_ref[...] / l_ref[...]


def _dsa_sample(aq, ak, av, iq, iw, ik, pool_k, pool_v, pool_kidx, page_table):
    bd, n_pages = page_table.shape
    n_pool = pool_k.shape[0]
    topk = min(TOPK_MAX, (n_pages * PAGE + 1) // 4)
    assert n_pages == LANE and n_pages % IDX_PAGES == 0 and n_pages % ATT_PAGES == 0
    iqt = jnp.zeros((bd, I_DIM, LANE), BF16).at[:, :, :I_HEADS].set(
        iq.reshape(bd, I_HEADS, I_DIM).transpose(0, 2, 1))
    w_row = jnp.where(jnp.arange(LANE) < I_HEADS, iw, 0.0).reshape(bd, 1, LANE)
    ik_new = jnp.zeros((bd, PAGE, I_DIM), F32).at[:, 0, :].set(ik)
    new_rows = lambda a: jnp.zeros((bd, PAGE_ROWS, A_DH), F32).at[:, :A_KV, :].set(a)
    k_new, v_new = new_rows(ak), new_rows(av)
    qt = jnp.zeros((bd, A_DH, LANE), BF16).at[:, :, :A_HEADS].set(
        aq.reshape(bd, A_HEADS, A_DH).transpose(0, 2, 1))

    def page_spec(rows, width, j, per):
        return pl.BlockSpec((1, rows, width), lambda b, g, pt: (pt[b, g * per + j], 0, 0))

    per_row = lambda shape: pl.BlockSpec((1,) + shape, lambda b, g, pt: (b, 0, 0))
    sel, sel_new = pl.pallas_call(
        functools.partial(_dsa_sample_index_kernel, n_pages=n_pages, topk=topk),
        grid_spec=pltpu.PrefetchScalarGridSpec(
            num_scalar_prefetch=1,
            grid=(bd, n_pages // IDX_PAGES),
            in_specs=[page_spec(PAGE, I_DIM, j, IDX_PAGES) for j in range(IDX_PAGES)]
            + [per_row((I_DIM, LANE)), per_row((1, LANE)), per_row((PAGE, I_DIM))],
            out_specs=[per_row((PAGE_ROWS, PAGE)), per_row((PAGE_ROWS, LANE))],
            scratch_shapes=[pltpu.VMEM((PAGE, LANE), F32)],
        ),
        out_shape=[jax.ShapeDtypeStruct((bd, PAGE_ROWS, PAGE), BF16),
                   jax.ShapeDtypeStruct((bd, PAGE_ROWS, LANE), F32)],
        compiler_params=_cparams(("parallel", "arbitrary")),
        name="dsa_sample_index",
    )(page_table, *([pool_kidx] * IDX_PAGES), iqt, w_row, ik_new)

    pk = pool_k.reshape(n_pool, PAGE_ROWS, A_DH)
    pv = pool_v.reshape(n_pool, PAGE_ROWS, A_DH)
    o_t = pl.pallas_call(
        functools.partial(_dsa_sample_attn_kernel, n_pages=n_pages),
        grid_spec=pltpu.PrefetchScalarGridSpec(
            num_scalar_prefetch=1,
            grid=(bd, n_pages // ATT_PAGES),
            in_specs=[page_spec(PAGE_ROWS, A_DH, j, ATT_PAGES) for j in range(ATT_PAGES)] * 2
            + [per_row((A_DH, LANE)), per_row((PAGE_ROWS, PAGE)), per_row((PAGE_ROWS, LANE)),
               per_row((PAGE_ROWS, A_DH)), per_row((PAGE_ROWS, A_DH))],
            out_specs=per_row((A_DH, LANE)),
            scratch_shapes=[pltpu.VMEM((1, LANE), F32), pltpu.VMEM((1, LANE), F32), pltpu.VMEM((A_DH, LANE), F32)],
        ),
        out_shape=jax.ShapeDtypeStruct((bd, A_DH, LANE), F32),
        compiler_params=_cparams(("parallel", "arbitrary")),
        name="dsa_sample_attn",
    )(page_table, *([pk] * ATT_PAGES), *([pv] * ATT_PAGES), qt, sel, sel_new, k_new, v_new)
    return o_t[:, :, :A_HEADS].transpose(0, 2, 1).reshape(bd, A_HEADS * A_DH).astype(BF16)


def _gate_mix_kernel(rn_ref, att_ref, wr_ref, wa_ref, gr_ref, ga_ref, o_ref):
    r = jnp.dot(rn_ref[...], wr_ref[...], preferred_element_type=F32)
    a = jnp.dot(att_ref[...], wa_ref[...], preferred_element_type=F32)
    o_ref[...] = (gr_ref[...].astype(F32) * r + ga_ref[...].astype(F32) * a).astype(o_ref.dtype)


def _out_proj_kernel(x_ref, mix_ref, wo_ref, nw_ref, h_ref, hn_ref):
    h = x_ref[...] + jnp.dot(mix_ref[...], wo_ref[...], preferred_element_type=F32)
    h_ref[...] = h
    ms = jnp.mean(h * h, axis=-1, keepdims=True)
    hn_ref[...] = (h * lax.rsqrt(ms + EPS) * nw_ref[...]).astype(hn_ref.dtype)


def _mixer_out(x2d, rn, att, gates, wr, wa, wo, norm2_w, tm):
    n, d = x2d.shape
    tn = 512
    ncol = d // tn
    mixed = pl.pallas_call(
        _gate_mix_kernel,
        grid=(ncol, n // tm),
        in_specs=[
            pl.BlockSpec((tm, rn.shape[1]), lambda j, i: (i, 0)),
            pl.BlockSpec((tm, att.shape[1]), lambda j, i: (i, 0)),
            pl.BlockSpec((wr.shape[0], tn), lambda j, i: (0, j)),
            pl.BlockSpec((wa.shape[0], tn), lambda j, i: (0, j)),
            pl.BlockSpec((tm, tn), lambda j, i: (i, j)),
            pl.BlockSpec((tm, tn), lambda j, i: (i, ncol + j)),
        ],
        out_specs=pl.BlockSpec((tm, tn), lambda j, i: (i, j)),
        out_shape=jax.ShapeDtypeStruct((n, d), BF16),
        compiler_params=_cparams(("parallel", "parallel")),
        name="gate_mix",
    )(rn, att, wr, wa, gates, gates)
    tm2 = min(tm, 256)
    h, hn = pl.pallas_call(
        _out_proj_kernel,
        grid=(n // tm2,),
        in_specs=[
            pl.BlockSpec((tm2, d), lambda i: (i, 0)),
            pl.BlockSpec((tm2, d), lambda i: (i, 0)),
            pl.BlockSpec((d, d), lambda i: (0, 0)),
            pl.BlockSpec((1, d), lambda i: (0, 0)),
        ],
        out_specs=[pl.BlockSpec((tm2, d), lambda i: (i, 0)), pl.BlockSpec((tm2, d), lambda i: (i, 0))],
        out_shape=[jax.ShapeDtypeStruct((n, d), F32), jax.ShapeDtypeStruct((n, d), BF16)],
        compiler_params=_cparams(("parallel",)),
        name="out_proj",
    )(x2d, mixed, wo, norm2_w.reshape(1, d))
    return h, hn


PEER_TE = 1024
NO_RANK = 127.0


def _top_desc(x, count):
    vals = []
    cur = x
    for _ in range(count):
        mx = jnp.max(cur, axis=0, keepdims=True)
        vals.append(mx)
        cur = jnp.where(cur == mx, -jnp.inf, cur)
    return vals


def _top_desc_rank(x, count):
    vals = []
    cur = x
    rank = jnp.full(x.shape, NO_RANK, F32)
    for i in range(count):
        mx = jnp.max(cur, axis=0, keepdims=True)
        hit = cur == mx
        vals.append(mx)
        rank = jnp.where(hit, float(i), rank)
        cur = jnp.where(hit, -jnp.inf, cur)
    return vals, rank


def _peer_route_kernel(q_ref, k1_ref, k2_ref, n1_ref, e1_ref, r2_ref, e2_ref):
    nt = (((1,), (1,)), ((), ()))
    half = P_TOPK // 2
    for h in range(P_HEADS):
        qh = q_ref[:, h * P_DKEY:(h + 1) * P_DKEY]
        s1 = lax.dot_general(k1_ref[h], qh, nt, preferred_element_type=F32)
        s2 = lax.dot_general(k2_ref[h], qh, nt, preferred_element_type=F32)
        v1 = _top_desc(s1, P_TOPK)
        v2, r2 = _top_desc_rank(s2, P_TOPK)
        v1blk = jnp.concatenate(v1, axis=0)
        v2blk = jnp.concatenate(v2, axis=0)
        blocks = ([v1[0] + v2blk] + [v1[i] + v2blk[:half] for i in range(1, half)]
                  + [v1blk[half:] + v2[0]])
        c = _top_desc(jnp.concatenate(blocks, axis=0), P_TOPK)
        thr = c[-1]
        z = jnp.zeros_like(thr)
        for i in range(P_TOPK):
            z = z + jnp.exp(c[i] - c[0])
        counts = [jnp.sum(jnp.where(blk >= thr, 1.0, 0.0), axis=0, keepdims=True) for blk in blocks[:half]]
        tail = jnp.where(blocks[half] >= thr, 1.0, 0.0)
        counts += [tail[i:i + 1] for i in range(half)]
        n1 = jnp.zeros_like(s1)
        for i in range(P_TOPK):
            n1 = jnp.where(s1 == v1[i], counts[i], n1)
        n1_ref[h] = n1
        e1_ref[h] = jnp.where(s1 >= v1[-1], jnp.exp(s1 - v1[0]) / z, 0.0)
        r2_ref[h] = r2.astype(r2_ref.dtype)
        e2_ref[h] = jnp.where(r2 < P_TOPK, jnp.exp(s2 - v2[0]), 0.0).astype(e2_ref.dtype)


def _peer_expert_kernel(hn_ref, h_ref, u_ref, vt_ref, n1_ref, e1_ref, r2_ref, e2_ref,
                        y_ref, acc_ref, w_ref, *, a_per_tile):
    j = pl.program_id(1)

    @pl.when(j == 0)
    def _():
        acc_ref[...] = jnp.zeros_like(acc_ref)

    tm = w_ref.shape[1]
    for al in range(a_per_tile):
        a = j * a_per_tile + al
        w = jnp.zeros((P_NKEYS, tm), BF16)
        for h in range(P_HEADS):
            nrow = jnp.broadcast_to(n1_ref[h, pl.ds(a, 1), :].astype(BF16), (P_NKEYS, tm))
            erow = jnp.broadcast_to(e1_ref[h, pl.ds(a, 1), :].astype(BF16), (P_NKEYS, tm))
            w = w + jnp.where(r2_ref[h] < nrow, e2_ref[h], jnp.zeros((), BF16)) * erow
        w_ref[al * P_NKEYS:(al + 1) * P_NKEYS, :] = w

    nt = (((1,), (1,)), ((), ()))
    x = lax.dot_general(u_ref[...], hn_ref[...], nt, preferred_element_type=F32)
    p = w_ref[...] * (0.5 * x * (1.0 + lax.erf(x * (2.0 ** -0.5)))).astype(BF16)
    acc_ref[...] += jnp.dot(vt_ref[...], p, preferred_element_type=F32)

    @pl.when(j == pl.num_programs(1) - 1)
    def _():
        y_ref[...] = h_ref[...] + acc_ref[...].T


def _peer(hn, h, wq_t, k1p, k2p, u_bf, vt_bf, tm):
    n, d = hn.shape
    te = PEER_TE
    hk = P_HEADS * P_DKEY
    tr = min(tm, 128)
    tok = lambda i: (0, 0, i)
    head_blk = pl.BlockSpec((P_HEADS, P_NKEYS, tr), tok)
    unused = jnp.zeros((tm, LANE), F32)
    q, = _proj(hn, wq_t, (0, hk), unused, unused, mode="plain", out_dtypes=(BF16,), tm=tm, tn=512)
    n1, e1, r2, e2 = pl.pallas_call(
        _peer_route_kernel,
        grid=(n // tr,),
        in_specs=[
            pl.BlockSpec((tr, hk), lambda i: (i, 0)),
            pl.BlockSpec((P_HEADS, P_NKEYS, P_DKEY), lambda i: (0, 0, 0)),
            pl.BlockSpec((P_HEADS, P_NKEYS, P_DKEY), lambda i: (0, 0, 0)),
        ],
        out_specs=[head_blk] * 4,
        out_shape=[jax.ShapeDtypeStruct((P_HEADS, P_NKEYS, n), dt) for dt in (F32, F32, BF16, BF16)],
        compiler_params=_cparams(("parallel",)),
        name="peer_route",
    )(q, k1p, k2p)

    n_exp = u_bf.shape[0]
    once = pl.Buffered(1)
    tok2 = lambda i, j: (0, 0, i)
    head_f32 = pl.BlockSpec((P_HEADS, P_NKEYS, tm), tok2, pipeline_mode=once)
    head_bf16 = pl.BlockSpec((P_HEADS, P_NKEYS, tm), tok2)
    return pl.pallas_call(
        functools.partial(_peer_expert_kernel, a_per_tile=te // P_NKEYS),
        grid=(n // tm, n_exp // te),
        in_specs=[
            pl.BlockSpec((tm, d), lambda i, j: (i, 0)),
            pl.BlockSpec((tm, d), lambda i, j: (i, 0), pipeline_mode=once),
            pl.BlockSpec((te, d), lambda i, j: (j, 0)),
            pl.BlockSpec((d, te), lambda i, j: (0, j)),
            head_f32, head_f32, head_bf16, head_bf16,
        ],
        out_specs=pl.BlockSpec((tm, d), lambda i, j: (i, 0)),
        out_shape=jax.ShapeDtypeStruct((n, d), F32),
        scratch_shapes=[pltpu.VMEM((d, tm), F32), pltpu.VMEM((te, tm), BF16)],
        compiler_params=_cparams(("parallel", "arbitrary")),
        name="peer_experts",
    )(hn, h, u_bf, vt_bf, n1, e1, r2, e2)


def _prep_weights(w_in, w_ret_o, w_att_o, w_out, peer_wq, peer_keys1, peer_keys2, peer_u, peer_v):
    half = P_DKEY // 2
    k1p = jnp.pad(peer_keys1, ((0, 0), (0, 0), (0, half))).astype(BF16)
    k2p = jnp.pad(peer_keys2, ((0, 0), (0, 0), (half, 0))).astype(BF16)
    return dict(w_t=w_in.T,
                wr=w_ret_o.astype(BF16), wa=w_att_o.astype(BF16), wo=w_out.astype(BF16),
                wq_t=peer_wq.T, k1p=k1p, k2p=k2p,
                u_bf=peer_u.astype(BF16), vt_bf=peer_v.T.astype(BF16))


def _prompt_layer(x, W, norm1_w, q_norm_w, k_norm_w, ret_gn_w, norm2_w):
    bsz, seq, d = x.shape
    x2d = x.reshape(bsz * seq, d)
    tm = 512
    p = _mixer_in(x2d, jnp.arange(seq), norm1_w, W["w_t"], q_norm_w, k_norm_w, 2 * tm)
    s0 = jnp.zeros((bsz, R_HEADS, R_DK, R_DV), F32)
    rn, S = _retention_prompt(p["rq"], p["rk"], p["rvg"], ret_gn_w, s0, bsz, seq)
    att = _dsa_prompt(p["iq"], p["iw"], p["ik_bf"], p["aq"], p["ak_bf"], p["av_bf"], bsz, seq)
    h, hn = _mixer_out(x2d, rn, att, p["gates"], W["wr"], W["wa"], W["wo"], norm2_w, tm)
    y = _peer(hn, h, W["wq_t"], W["k1p"], W["k2p"], W["u_bf"], W["vt_bf"], tm=512)
    return (y.reshape(bsz, seq, d), p["ak"].reshape(bsz, seq, A_KV, A_DH), p["av"].reshape(bsz, seq, A_KV, A_DH),
            p["ik"].reshape(bsz, seq, I_DIM), S)


def _sample_layer(x, pool_k, pool_v, pool_kidx, s_in, page_table, W, norm1_w, q_norm_w, k_norm_w,
                  ret_gn_w, norm2_w):
    bd, ts, d = x.shape
    assert ts == 1
    x2d = x.reshape(bd, d)
    past = page_table.shape[1] * PAGE
    p = _mixer_in(x2d, jnp.full((bd,), past, I32), norm1_w, W["w_t"], q_norm_w, k_norm_w, bd)
    rn, S = _retention_sample(p["rq"], p["rk"], p["rvg"], ret_gn_w, s_in)
    att = _dsa_sample(p["aq"], p["ak"], p["av"], p["iq"], p["iw"], p["ik"], pool_k, pool_v, pool_kidx, page_table)
    h, hn = _mixer_out(x2d, rn, att, p["gates"], W["wr"], W["wa"], W["wo"], norm2_w, bd)
    pad = LANE - bd
    hn_p = jnp.pad(hn, ((0, pad), (0, 0)))
    h_p = jnp.pad(h, ((0, pad), (0, 0)))
    y = _peer(hn_p, h_p, W["wq_t"], W["k1p"], W["k2p"], W["u_bf"], W["vt_bf"], tm=LANE)[:bd]
    return (y.reshape(bd, ts, d), p["ak"].reshape(bd, ts, A_KV, A_DH), p["av"].reshape(bd, ts, A_KV, A_DH),
            p["ik"].reshape(bd, ts, I_DIM), S)


def kernel(x_prompt, x_sample, cache_k, cache_v, cache_k_idx, state_ret, page_table, norm1_w, w_in, q_norm_w,
           k_norm_w, ret_gn_w, w_ret_o, w_att_o, w_out, norm2_w, peer_wq, peer_keys1, peer_keys2, peer_u, peer_v):
    depth = w_in.shape[0]
    assert depth == 1
    l = 0
    W = _prep_weights(w_in[l], w_ret_o[l], w_att_o[l], w_out[l], peer_wq[l], peer_keys1[l], peer_keys2[l],
                      peer_u[l], peer_v[l])
    yp, kp, vp, ikp, sp = _prompt_layer(x_prompt, W, norm1_w[l], q_norm_w[l], k_norm_w[l], ret_gn_w[l], norm2_w[l])
    ys, kn, vn, ikn, sn = _sample_layer(x_sample, cache_k[l], cache_v[l], cache_k_idx[l], state_ret[l], page_table,
                                        W, norm1_w[l], q_norm_w[l], k_norm_w[l], ret_gn_w[l], norm2_w[l])
    return (yp, ys, kp[None], vp[None], ikp[None], sp[None], kn[None], vn[None], ikn[None], sn[None])
```

```python
import functools

import jax
import jax.numpy as jnp
from jax import lax
from jax.experimental import pallas as pl
from jax.experimental.pallas import tpu as pltpu

F32 = jnp.float32
BF16 = jnp.bfloat16
I32 = jnp.int32

D_MODEL = 2048
PAGE = 128
R_HEADS, R_DK, R_DV, R_CHUNK = 8, 128, 256, 128
A_HEADS, A_KV, A_DH = 16, 4, 128
A_GROUP = A_HEADS // A_KV
I_HEADS, I_DIM = 16, 128
TOPK_MAX = 256
Q_BLOCK = 128
ROPE_THETA = 10000.0
P_HEADS, P_NKEYS, P_DKEY, P_TOPK = 8, 128, 128, 16
EPS = 1e-6

LANE = 128
VMEM_LIMIT = 56 * 1024 * 1024
NEG_BIG = -1e30
LOG2E = 1.4426950408889634
INT_MIN = -2147483648
NEGINF_KEY = INT_MIN + 0x7FFFFF

SEG = dict(rq=(0, 1024), rk=(1024, 1024), rvg=(2048, 4096), aq=(6144, 2048), ak=(8192, 512), av=(8704, 512),
           iq=(9216, 2048), ik=(11264, 128), iw=(11392, 128))
GATE_COL = 11408
GATE_WIDTH = 2 * D_MODEL


def _cparams(sem):
    return pltpu.CompilerParams(dimension_semantics=sem, vmem_limit_bytes=VMEM_LIMIT)


def _rmsnorm_kernel(x_ref, w_ref, o_ref):
    x = x_ref[...]
    ms = jnp.mean(x * x, axis=-1, keepdims=True)
    o_ref[...] = (x * lax.rsqrt(ms + EPS) * w_ref[...]).astype(o_ref.dtype)


def _rmsnorm(x, w, tm):
    n, d = x.shape
    return pl.pallas_call(
        _rmsnorm_kernel,
        grid=(n // tm,),
        in_specs=[pl.BlockSpec((tm, d), lambda i: (i, 0)), pl.BlockSpec((1, d), lambda i: (0, 0))],
        out_specs=pl.BlockSpec((tm, d), lambda i: (i, 0)),
        out_shape=jax.ShapeDtypeStruct((n, d), BF16),
        compiler_params=_cparams(("parallel",)),
        name="rmsnorm",
    )(x, w.reshape(1, d))


def _proj_kernel(x_ref, w_ref, a_ref, b_ref, *refs, mode, scale, heads):
    out_refs, wb_ref = refs[:-1], refs[-1]

    @pl.when(pl.program_id(1) == 0)
    def _():
        wb_ref[...] = w_ref[...].astype(wb_ref.dtype)

    z = lax.dot_general(x_ref[...], wb_ref[...], (((1,), (1,)), ((), ())), preferred_element_type=F32)
    if mode in ("rope", "norm_rope"):
        a, b = a_ref[...], b_ref[...]
        parts = []
        for h in range(heads):
            zh = z[:, h * LANE:(h + 1) * LANE]
            y = zh * a + pltpu.roll(zh, LANE // 2, 1) * b
            if mode == "norm_rope":
                sq = zh * zh
                hi = sq.astype(BF16)
                lo = (sq - hi.astype(F32)).astype(BF16)
                avg = jnp.full((LANE, LANE), 1.0 / LANE, BF16)
                ms = (jnp.dot(hi, avg, preferred_element_type=F32) + jnp.dot(lo, avg, preferred_element_type=F32))
                y = y * lax.rsqrt(ms + EPS)
            parts.append(y)
        z = jnp.concatenate(parts, axis=1) if heads > 1 else parts[0]
    elif mode == "sigmoid":
        z = jax.nn.sigmoid(z)
    elif scale != 1.0:
        z = z * scale
    for o in out_refs:
        if len(o.shape) == 3:
            for h in range(heads):
                o[:, h, :] = z[:, h * LANE:(h + 1) * LANE].astype(o.dtype)
        else:
            o[...] = z.astype(o.dtype)


def _proj(xn, w_t, seg, a_tab, b_tab, *, mode, scale=1.0, out_dtypes, tm, tn, split_first=False):
    n, d = xn.shape
    col0, width = seg
    tn = min(tn, width)
    assert col0 % 8 == 0 and width % tn == 0 and n % tm == 0 and not (split_first and tn != width)
    pos_blocks = a_tab.shape[0] // tm
    out_blocks = [pl.BlockSpec((tm, tn), lambda j, i: (i, j)) for _ in out_dtypes]
    out_shapes = [jax.ShapeDtypeStruct((n, width), dt) for dt in out_dtypes]
    if split_first:
        out_blocks[0] = pl.BlockSpec((tm, width // LANE, LANE), lambda j, i: (i, 0, 0))
        out_shapes[0] = jax.ShapeDtypeStruct((n, width // LANE, LANE), out_dtypes[0])
    kern = functools.partial(_proj_kernel, mode=mode, scale=scale, heads=tn // LANE)
    outs = pl.pallas_call(
        kern,
        grid=(width // tn, n // tm),
        in_specs=[
            pl.BlockSpec((tm, d), lambda j, i: (i, 0)),
            pl.BlockSpec((pl.Element(tn), pl.Element(d)), lambda j, i: (pl.multiple_of(col0 + j * tn, 8), 0)),
            pl.BlockSpec((tm, LANE), lambda j, i: (i % pos_blocks, 0)),
            pl.BlockSpec((tm, LANE), lambda j, i: (i % pos_blocks, 0)),
        ],
        out_specs=out_blocks,
        out_shape=out_shapes,
        scratch_shapes=[pltpu.VMEM((tn, d), BF16)],
        compiler_params=_cparams(("parallel", "arbitrary")),
        name="proj_" + mode,
    )(xn, w_t, a_tab, b_tab)
    return outs


def _rope_tables(pos):
    half = LANE // 2
    freqs = ROPE_THETA ** (-jnp.arange(half, dtype=F32) / half)
    ang = pos.astype(F32)[:, None] * freqs[None, :]
    cos, sin = jnp.cos(ang), jnp.sin(ang)
    return jnp.concatenate([cos, cos], axis=1), jnp.concatenate([-sin, sin], axis=1)


def _mixer_in(x2d, pos_rows, norm1_w, w_t, q_norm_w, k_norm_w, tm):
    xn = _rmsnorm(x2d, norm1_w, tm)
    cos2, sin2 = _rope_tables(pos_rows)
    swap = lambda g: jnp.roll(g, LANE // 2)
    rope = lambda gain, scale: (cos2 * (gain * scale)[None, :], sin2 * (swap(gain) * scale)[None, :])
    ones = jnp.ones((LANE,), F32)
    p = functools.partial(_proj, xn, tm=tm, tn=512)
    wide = functools.partial(_proj, xn, tm=tm, tn=1024)
    out = {}
    out["rq"], = p(w_t, SEG["rq"], *rope(ones, 1.0), mode="rope", out_dtypes=(BF16,))
    out["rk"], = p(w_t, SEG["rk"], *rope(ones, R_DK ** -0.5), mode="rope", out_dtypes=(BF16,))
    out["rvg"], = wide(w_t, SEG["rvg"], cos2, sin2, mode="plain", out_dtypes=(BF16,))
    out["aq"], = p(w_t, SEG["aq"], *rope(q_norm_w, A_DH ** -0.5 * LOG2E), mode="norm_rope", out_dtypes=(BF16,))
    out["ak"], out["ak_bf"] = p(w_t, SEG["ak"], *rope(k_norm_w, 1.0), mode="norm_rope", out_dtypes=(F32, BF16),
                                 split_first=True)
    out["av"], out["av_bf"] = p(w_t, SEG["av"], cos2, sin2, mode="plain", out_dtypes=(F32, BF16), split_first=True)
    out["iq"], = p(w_t, SEG["iq"], *rope(ones, I_DIM ** -0.5), mode="rope", out_dtypes=(BF16,))
    out["ik"], out["ik_bf"] = p(w_t, SEG["ik"], *rope(ones, 1.0), mode="rope", out_dtypes=(F32, BF16))
    out["iw"], = p(w_t, SEG["iw"], cos2, sin2, mode="plain", scale=I_HEADS ** -0.5, out_dtypes=(F32,))
    out["gates"], = wide(w_t, (GATE_COL, GATE_WIDTH), cos2, sin2, mode="sigmoid", out_dtypes=(BF16,))
    return out


def _ret_tables(chunk):
    log_g = jnp.log1p(-jnp.exp2(-5.0 - jnp.arange(R_HEADS, dtype=F32)))
    n = jnp.arange(chunk, dtype=F32)
    diff = n[:, None] - n[None, :]
    causal = diff >= 0
    dmat = jnp.where(causal, jnp.exp(jnp.where(causal, diff, 0.0)[None] * log_g[:, None, None]), 0.0)
    qdec = jnp.exp((n + 1.0)[None, :] * log_g[:, None])
    kdec = jnp.exp((chunk - 1.0 - n)[None, :] * log_g[:, None])
    gc = jnp.exp(chunk * log_g)
    return dmat, qdec, kdec, gc


def _ret_kernel(q_ref, k_ref, v_ref, g_ref, gnw_ref, dmat_ref, qdec_ref, kdec_ref, gc_ref, s0_ref,
                o_ref, s_ref):
    c = pl.program_id(1)

    @pl.when(c == 0)
    def _():
        s_ref[...] = s0_ref[...]

    nt = (((1,), (1,)), ((), ()))
    tn = (((0,), (0,)), ((), ()))
    for h in range(R_HEADS):
        S = s_ref[0, h]
        q = q_ref[:, h * R_DK:(h + 1) * R_DK]
        k = k_ref[:, h * R_DK:(h + 1) * R_DK]
        v = v_ref[:, h * R_DV:(h + 1) * R_DV]
        scores = lax.dot_general(q, k, nt, preferred_element_type=F32) * dmat_ref[h]
        intra = jnp.dot(scores.astype(BF16), v, preferred_element_type=F32)
        qd = (q.astype(F32) * qdec_ref[h]).astype(BF16)
        cross = jnp.dot(qd, S.astype(BF16), preferred_element_type=F32)
        kd = (k.astype(F32) * kdec_ref[h]).astype(BF16)
        s_ref[0, h] = gc_ref[h] * S + lax.dot_general(kd, v, tn, preferred_element_type=F32)
        o = intra + cross
        mu = jnp.mean(o, axis=-1, keepdims=True)
        var = jnp.mean(jnp.square(o - mu), axis=-1, keepdims=True)
        g = g_ref[:, h * R_DV:(h + 1) * R_DV].astype(F32)
        rn = (o - mu) * lax.rsqrt(var + EPS) * gnw_ref[:, h * R_DV:(h + 1) * R_DV] * (g * jax.nn.sigmoid(g))
        o_ref[:, h * R_DV:(h + 1) * R_DV] = rn.astype(o_ref.dtype)


def _retention_prompt(rq, rk, rvg, gn_w, s0, bsz, seq):
    C = R_CHUNK
    nc = seq // C
    dmat, qdec, kdec, gc = _ret_tables(C)
    qdec = jnp.broadcast_to(qdec[:, :, None], (R_HEADS, C, R_DK))
    kdec = jnp.broadcast_to(kdec[:, :, None], (R_HEADS, C, R_DK))
    gc = jnp.broadcast_to(gc[:, None, None], (R_HEADS, 1, R_DV))
    hk, hv = R_HEADS * R_DK, R_HEADS * R_DV
    row = lambda b, c: (b * nc + c, 0)
    const3 = lambda b, c: (0, 0, 0)
    state = pl.BlockSpec((1, R_HEADS, R_DK, R_DV), lambda b, c: (b, 0, 0, 0))
    rn, S = pl.pallas_call(
        _ret_kernel,
        grid=(bsz, nc),
        in_specs=[
            pl.BlockSpec((C, hk), row),
            pl.BlockSpec((C, hk), row),
            pl.BlockSpec((C, hv), row),
            pl.BlockSpec((C, hv), lambda b, c: (b * nc + c, 1)),
            pl.BlockSpec((1, hv), lambda b, c: (0, 0)),
            pl.BlockSpec((R_HEADS, C, C), const3),
            pl.BlockSpec((R_HEADS, C, R_DK), const3),
            pl.BlockSpec((R_HEADS, C, R_DK), const3),
            pl.BlockSpec((R_HEADS, 1, R_DV), const3),
            state,
        ],
        out_specs=[pl.BlockSpec((C, hv), row), state],
        out_shape=[jax.ShapeDtypeStruct((bsz * seq, hv), BF16),
                   jax.ShapeDtypeStruct((bsz, R_HEADS, R_DK, R_DV), F32)],
        compiler_params=_cparams(("parallel", "arbitrary")),
        name="retention",
    )(rq, rk, rvg, rvg, gn_w.reshape(1, -1), dmat, qdec, kdec, gc, s0)
    return rn, S


def _ret_step_kernel(qc_ref, kc_ref, v_ref, g_ref, gnw_ref, gam_ref, s0_ref, o_ref, s_ref):
    for h in range(R_HEADS):
        S = s0_ref[0, h]
        qc, kc = qc_ref[0, h], kc_ref[0, h]
        v = v_ref[0, h]
        gam = gam_ref[h]
        qk = jnp.sum(qc * kc, axis=0, keepdims=True)
        cross = jnp.sum((qc * gam) * S, axis=0, keepdims=True)
        o = qk * v + cross
        s_ref[0, h] = gam * S + kc * v
        mu = jnp.mean(o, axis=-1, keepdims=True)
        var = jnp.mean(jnp.square(o - mu), axis=-1, keepdims=True)
        g = g_ref[0, h]
        o_ref[0, h] = (o - mu) * lax.rsqrt(var + EPS) * gnw_ref[0, h] * (g * jax.nn.sigmoid(g))


def _retention_sample(rq, rk, rvg, gn_w, s0):
    bd = rq.shape[0]
    _, qdec, _, _ = _ret_tables(1)
    gam = jnp.broadcast_to(qdec[:, :, None], (R_HEADS, 1, R_DV)).astype(F32)
    qc = rq.astype(F32).reshape(bd, R_HEADS, R_DK, 1)
    kc = rk.astype(F32).reshape(bd, R_HEADS, R_DK, 1)
    rv = rvg[:, :R_HEADS * R_DV].astype(F32).reshape(bd, R_HEADS, 1, R_DV)
    rg = rvg[:, R_HEADS * R_DV:].astype(F32).reshape(bd, R_HEADS, 1, R_DV)
    gnw = gn_w.reshape(1, R_HEADS, 1, R_DV)
    col = pl.BlockSpec((1, R_HEADS, R_DK, 1), lambda b: (b, 0, 0, 0))
    rowv = pl.BlockSpec((1, R_HEADS, 1, R_DV), lambda b: (b, 0, 0, 0))
    st = pl.BlockSpec((1, R_HEADS, R_DK, R_DV), lambda b: (b, 0, 0, 0))
    rn, S = pl.pallas_call(
        _ret_step_kernel,
        grid=(bd,),
        in_specs=[col, col, rowv, rowv,
                  pl.BlockSpec((1, R_HEADS, 1, R_DV), lambda b: (0, 0, 0, 0)),
                  pl.BlockSpec((R_HEADS, 1, R_DV), lambda b: (0, 0, 0)), st],
        out_specs=[rowv, st],
        out_shape=[jax.ShapeDtypeStruct((bd, R_HEADS, 1, R_DV), F32),
                   jax.ShapeDtypeStruct((bd, R_HEADS, R_DK, R_DV), F32)],
        compiler_params=_cparams(("parallel",)),
        name="retention_step",
    )(qc, kc, rv, rg, gnw, gam, s0)
    return rn.reshape(bd, R_HEADS * R_DV).astype(BF16), S


def _key_to_float(key):
    return pltpu.bitcast(key ^ ((key >> 31) & 0x7FFFFFFF), F32)


def _kth_largest(count_ge, k, shape):
    cur = jnp.where(count_ge(jnp.zeros(shape, F32)) >= k, 0, INT_MIN).astype(I32)

    def body(i, cur):
        cand = cur | jnp.left_shift(jnp.int32(1), 30 - i)
        return jnp.where(count_ge(_key_to_float(cand)) >= k, cand, cur)

    key = lax.fori_loop(0, 31, body, cur)
    return jnp.where(key <= NEGINF_KEY, jnp.finfo(F32).min, _key_to_float(jnp.maximum(key, NEGINF_KEY + 1)))


KEY_UNIT = 256
SEARCH_KEYS = 512
TRIP_KEYS = 1024


def _dsa_prompt_kernel(iq_ref, iw_ref, ik_ref, aq_ref, ak_ref, av_ref, o_ref,
                       keys_ref, bias_ref, m_ref, acc_ref, *, topk):
    qb = pl.program_id(1)
    ntrips = (qb * Q_BLOCK + Q_BLOCK + TRIP_KEYS - 1) // TRIP_KEYS
    nt = (((1,), (1,)), ((), ()))

    iq = iq_ref[...]
    pair = lambda hp: jnp.concatenate([iq[:, (2 * hp) * LANE:(2 * hp + 1) * LANE],
                                       iq[:, (2 * hp + 1) * LANE:(2 * hp + 2) * LANE]], axis=0)
    rhs = [pair(hp) for hp in range(I_HEADS // 2)]
    w_t = iw_ref[...].T
    t_idx = qb * Q_BLOCK + lax.broadcasted_iota(I32, (Q_BLOCK, Q_BLOCK), 1)
    row_iota = lax.broadcasted_iota(I32, (Q_BLOCK, Q_BLOCK), 0)

    def score_trip(t, carry):
        for c in range(TRIP_KEYS // Q_BLOCK):
            off = pl.multiple_of(t * TRIP_KEYS + c * Q_BLOCK, Q_BLOCK)
            ikc = ik_ref[0, pl.ds(off, Q_BLOCK), :]
            sc = jnp.zeros((Q_BLOCK, Q_BLOCK), F32)
            for hp in range(I_HEADS // 2):
                prod = lax.dot_general(ikc, rhs[hp], nt, preferred_element_type=F32)
                sc = sc + (jnp.maximum(prod[:, :Q_BLOCK], 0.0) * w_t[2 * hp:2 * hp + 1, :]
                           + jnp.maximum(prod[:, Q_BLOCK:], 0.0) * w_t[2 * hp + 1:2 * hp + 2, :])
            sc = jnp.where(off + row_iota <= t_idx, sc, -jnp.inf)
            keys_ref[pl.ds(off, Q_BLOCK), :] = sc
        return carry

    lax.fori_loop(0, ntrips, score_trip, 0)

    nsearch = (qb * Q_BLOCK + Q_BLOCK + SEARCH_KEYS - 1) // SEARCH_KEYS

    def count_ge(cand):
        def body(u, acc):
            for c in range(SEARCH_KEYS // KEY_UNIT):
                off = pl.multiple_of(u * SEARCH_KEYS + c * KEY_UNIT, KEY_UNIT)
                blk = keys_ref[pl.ds(off, KEY_UNIT), :]
                hit = jnp.where(blk >= cand, 1, 0).astype(I32)
                acc = acc + jnp.sum(hit.reshape(KEY_UNIT // 8, 8, Q_BLOCK), axis=0)
            return acc
        acc = lax.fori_loop(0, nsearch, body, jnp.zeros((8, Q_BLOCK), I32))
        return jnp.sum(acc, axis=0, keepdims=True)

    thr = _kth_largest(count_ge, topk, (1, Q_BLOCK))

    def bias_trip(t, carry):
        for c in range(TRIP_KEYS // Q_BLOCK):
            off = pl.multiple_of(t * TRIP_KEYS + c * Q_BLOCK, Q_BLOCK)
            sel = keys_ref[pl.ds(off, Q_BLOCK), :] >= thr
            bias_ref[:, pl.ds(off, Q_BLOCK)] = jnp.where(sel, 0.0, NEG_BIG).astype(F32).T
        return carry

    lax.fori_loop(0, ntrips, bias_trip, 0)

    rows = A_GROUP * Q_BLOCK
    ones_v = jnp.ones((KEY_UNIT, A_DH), BF16)
    subs = TRIP_KEYS // KEY_UNIT

    def logits(n, off):
        qg = jnp.concatenate([aq_ref[:, (n * A_GROUP + g) * A_DH:(n * A_GROUP + g + 1) * A_DH]
                              for g in range(A_GROUP)], axis=0)
        kch = ak_ref[0, pl.ds(off, KEY_UNIT), n * A_DH:(n + 1) * A_DH]
        b = bias_ref[:, pl.ds(off, KEY_UNIT)]
        s = lax.dot_general(qg, kch, nt, preferred_element_type=F32)
        return s + jnp.concatenate([b] * A_GROUP, axis=0)

    m_ref[...] = jnp.full((A_KV, rows, LANE), NEG_BIG, F32)
    acc_ref[...] = jnp.zeros((A_KV, rows, 2 * A_DH), F32)

    def attend(t, carry):
        for n in range(A_KV):
            offs = [pl.multiple_of(t * TRIP_KEYS + c * KEY_UNIT, KEY_UNIT) for c in range(subs)]
            s = [logits(n, off) for off in offs]
            m_old = m_ref[n]
            m_new = m_old
            for sc in s:
                m_new = jnp.maximum(m_new, jnp.max(sc, axis=-1, keepdims=True))
            alpha = jnp.exp2(m_old - m_new)
            m = jnp.concatenate([m_new] * (KEY_UNIT // LANE), axis=1)
            pv = jnp.zeros((rows, 2 * A_DH), F32)
            for sc, off in zip(s, offs):
                p = jnp.exp2(sc - m).astype(BF16)
                vch = av_ref[0, pl.ds(off, KEY_UNIT), n * A_DH:(n + 1) * A_DH]
                pv = pv + jnp.dot(p, jnp.concatenate([vch, ones_v], axis=1), preferred_element_type=F32)
            acc_ref[n] = acc_ref[n] * jnp.concatenate([alpha, alpha], axis=1) + pv
            m_ref[n] = m_new
        return carry

    lax.fori_loop(0, ntrips, attend, 0)
    for n in range(A_KV):
        acc = acc_ref[n]
        o = acc[:, :A_DH] / acc[:, A_DH:]
        for g in range(A_GROUP):
            hh = n * A_GROUP + g
            o_ref[:, hh * A_DH:(hh + 1) * A_DH] = o[g * Q_BLOCK:(g + 1) * Q_BLOCK].astype(o_ref.dtype)


def _dsa_prompt(iq, iw, ik_bf, aq, ak_bf, av_bf, bsz, seq):
    nb = seq // Q_BLOCK
    topk = min(TOPK_MAX, seq // 4)
    assert seq % TRIP_KEYS == 0
    ik3 = ik_bf.reshape(bsz, seq, I_DIM)
    ak3 = ak_bf.reshape(bsz, seq, A_KV * A_DH)
    av3 = av_bf.reshape(bsz, seq, A_KV * A_DH)
    rowblk = lambda b, q: (b * nb + q, 0)
    rows = A_GROUP * Q_BLOCK
    return pl.pallas_call(
        functools.partial(_dsa_prompt_kernel, topk=topk),
        grid=(bsz, nb),
        in_specs=[
            pl.BlockSpec((Q_BLOCK, I_HEADS * I_DIM), rowblk),
            pl.BlockSpec((Q_BLOCK, LANE), rowblk),
            pl.BlockSpec((1, seq, I_DIM), lambda b, q: (b, 0, 0)),
            pl.BlockSpec((Q_BLOCK, A_HEADS * A_DH), rowblk),
            pl.BlockSpec((1, seq, A_KV * A_DH), lambda b, q: (b, 0, 0)),
            pl.BlockSpec((1, seq, A_KV * A_DH), lambda b, q: (b, 0, 0)),
        ],
        out_specs=pl.BlockSpec((Q_BLOCK, A_HEADS * A_DH), rowblk),
        out_shape=jax.ShapeDtypeStruct((bsz * seq, A_HEADS * A_DH), BF16),
        scratch_shapes=[
            pltpu.VMEM((seq, Q_BLOCK), F32),
            pltpu.VMEM((Q_BLOCK, seq), F32),
            pltpu.VMEM((A_KV, rows, LANE), F32),
            pltpu.VMEM((A_KV, rows, 2 * A_DH), F32),
        ],
        compiler_params=_cparams(("parallel", "arbitrary")),
        name="dsa_prompt",
    )(iq, iw, ik3, aq, ak3, av3)


IDX_PAGES = 16
ATT_PAGES = 16
PAGE_ROWS = PAGE * A_KV


def _dsa_sample_index_kernel(pt_ref, *refs, n_pages, topk):
    page_refs = refs[:IDX_PAGES]
    iqt_ref, w_ref, iknew_ref, sel_ref, selnew_ref, sc_ref = refs[IDX_PAGES:]
    g = pl.program_id(1)

    @pl.when(g == 0)
    def _():
        sc_ref[...] = jnp.zeros_like(sc_ref)

    iqt = iqt_ref[0]
    w = w_ref[0]
    lane = lax.broadcasted_iota(I32, (PAGE, LANE), 1)

    def col_score(keys_f32):
        s = jnp.dot(keys_f32.astype(BF16), iqt, preferred_element_type=F32)
        return jnp.sum(jnp.maximum(s, 0.0) * w, axis=1, keepdims=True)

    acc = sc_ref[...]
    for j in range(IDX_PAGES):
        col = col_score(page_refs[j][0])
        acc = acc + jnp.where(lane == g * IDX_PAGES + j, col, 0.0)
    sc_ref[...] = acc

    @pl.when(g == n_pages // IDX_PAGES - 1)
    def _():
        past = sc_ref[...]
        row = lax.broadcasted_iota(I32, (PAGE, 1), 0)
        new = jnp.where(row == 0, col_score(iknew_ref[0]), -jnp.inf)
        newk = new

        def count_ge(cand):
            hits = jnp.sum(jnp.where(past >= cand, 1, 0).astype(I32), axis=0, keepdims=True)
            hits = jnp.sum(hits, axis=1, keepdims=True)
            return hits + jnp.sum(jnp.where(newk >= cand, 1, 0).astype(I32), axis=0, keepdims=True)

        thr = _kth_largest(count_ge, topk, (1, 1))
        sel = jnp.where(past >= thr, 1.0, 0.0).astype(BF16)
        r = lax.broadcasted_iota(I32, (PAGE_ROWS, PAGE), 0)
        k = lax.broadcasted_iota(I32, (PAGE_ROWS, PAGE), 1)
        expand = jnp.where(r // A_KV == k, 1.0, 0.0).astype(BF16)
        sel_ref[0] = jnp.dot(expand, sel, preferred_element_type=F32).astype(sel_ref.dtype)
        new_sel = jnp.where(newk[0:1, :] >= thr, 1.0, 0.0)
        rr = lax.broadcasted_iota(I32, (PAGE_ROWS, LANE), 0)
        selnew_ref[0] = jnp.where(rr < A_KV, new_sel, 0.0)


def _dsa_sample_attn_kernel(pt_ref, *refs, n_pages):
    k_refs = refs[:ATT_PAGES]
    v_refs = refs[ATT_PAGES:2 * ATT_PAGES]
    qt_ref, sel_ref, selnew_ref, knew_ref, vnew_ref, o_ref, m_ref, l_ref, acc_ref = refs[2 * ATT_PAGES:]
    g = pl.program_id(1)

    @pl.when(g == 0)
    def _():
        m_ref[...] = jnp.full(m_ref.shape, NEG_BIG, F32)
        l_ref[...] = jnp.zeros_like(l_ref)
        acc_ref[...] = jnp.zeros_like(acc_ref)

    qt = qt_ref[0]
    r = lax.broadcasted_iota(I32, (PAGE_ROWS, LANE), 0)
    lane = lax.broadcasted_iota(I32, (PAGE_ROWS, LANE), 1)
    head_bias = jnp.where(r % A_KV == lane // A_GROUP, 0.0, NEG_BIG)
    pick_r = lax.broadcasted_iota(I32, (PAGE, LANE), 0)
    tn = (((0,), (0,)), ((), ()))

    def step(pages):
        s = [jnp.dot(k.astype(BF16), qt, preferred_element_type=F32) + head_bias + (sel_b - 1.0) * (-NEG_BIG)
             for k, _, sel_b in pages]
        m_old = m_ref[...]
        m_new = m_old
        for sc in s:
            m_new = jnp.maximum(m_new, jnp.max(sc, axis=0, keepdims=True))
        alpha = jnp.exp2(m_old - m_new)
        l = alpha * l_ref[...]
        acc = alpha * acc_ref[...]
        for sc, (_, v, _) in zip(s, pages):
            p = jnp.exp2(sc - m_new)
            l = l + jnp.sum(p, axis=0, keepdims=True)
            acc = acc + lax.dot_general(v.astype(BF16), p.astype(BF16), tn, preferred_element_type=F32)
        l_ref[...] = l
        acc_ref[...] = acc
        m_ref[...] = m_new

    sel = sel_ref[0]
    pages = []
    for j in range(ATT_PAGES):
        onehot = jnp.where(pick_r == g * ATT_PAGES + j, 1.0, 0.0).astype(BF16)
        sel_b = jnp.dot(sel, onehot, preferred_element_type=F32)
        pages.append((k_refs[j][0], v_refs[j][0], sel_b))
    step(pages)

    @pl.when(g == n_pages // ATT_PAGES - 1)
    def _():
        step([(knew_ref[0], vnew_ref[0], selnew_ref[0])])
        o_ref[0] = acc_ref[...] / l_ref[...]


def _dsa_sample(aq, ak, av, iq, iw, ik, pool_k, pool_v, pool_kidx, page_table):
    bd, n_pages = page_table.shape
    n_pool = pool_k.shape[0]
    topk = min(TOPK_MAX, (n_pages * PAGE + 1) // 4)
    assert n_pages == LANE and n_pages % IDX_PAGES == 0 and n_pages % ATT_PAGES == 0
    iqt = jnp.zeros((bd, I_DIM, LANE), BF16).at[:, :, :I_HEADS].set(
        iq.reshape(bd, I_HEADS, I_DIM).transpose(0, 2, 1))
    w_row = jnp.where(jnp.arange(LANE) < I_HEADS, iw, 0.0).reshape(bd, 1, LANE)
    ik_new = jnp.zeros((bd, PAGE, I_DIM), F32).at[:, 0, :].set(ik)
    new_rows = lambda a: jnp.zeros((bd, PAGE_ROWS, A_DH), F32).at[:, :A_KV, :].set(a)
    k_new, v_new = new_rows(ak), new_rows(av)
    qt = jnp.zeros((bd, A_DH, LANE), BF16).at[:, :, :A_HEADS].set(
        aq.reshape(bd, A_HEADS, A_DH).transpose(0, 2, 1))

    def page_spec(rows, width, j, per):
        return pl.BlockSpec((1, rows, width), lambda b, g, pt: (pt[b, g * per + j], 0, 0))

    per_row = lambda shape: pl.BlockSpec((1,) + shape, lambda b, g, pt: (b, 0, 0))
    sel, sel_new = pl.pallas_call(
        functools.partial(_dsa_sample_index_kernel, n_pages=n_pages, topk=topk),
        grid_spec=pltpu.PrefetchScalarGridSpec(
            num_scalar_prefetch=1,
            grid=(bd, n_pages // IDX_PAGES),
            in_specs=[page_spec(PAGE, I_DIM, j, IDX_PAGES) for j in range(IDX_PAGES)]
            + [per_row((I_DIM, LANE)), per_row((1, LANE)), per_row((PAGE, I_DIM))],
            out_specs=[per_row((PAGE_ROWS, PAGE)), per_row((PAGE_ROWS, LANE))],
            scratch_shapes=[pltpu.VMEM((PAGE, LANE), F32)],
        ),
        out_shape=[jax.ShapeDtypeStruct((bd, PAGE_ROWS, PAGE), BF16),
                   jax.ShapeDtypeStruct((bd, PAGE_ROWS, LANE), F32)],
        compiler_params=_cparams(("parallel", "arbitrary")),
        name="dsa_sample_index",
    )(page_table, *([pool_kidx] * IDX_PAGES), iqt, w_row, ik_new)

    pk = pool_k.reshape(n_pool, PAGE_ROWS, A_DH)
    pv = pool_v.reshape(n_pool, PAGE_ROWS, A_DH)
    o_t = pl.pallas_call(
        functools.partial(_dsa_sample_attn_kernel, n_pages=n_pages),
        grid_spec=pltpu.PrefetchScalarGridSpec(
            num_scalar_prefetch=1,
            grid=(bd, n_pages // ATT_PAGES),
            in_specs=[page_spec(PAGE_ROWS, A_DH, j, ATT_PAGES) for j in range(ATT_PAGES)] * 2
            + [per_row((A_DH, LANE)), per_row((PAGE_ROWS, PAGE)), per_row((PAGE_ROWS, LANE)),
               per_row((PAGE_ROWS, A_DH)), per_row((PAGE_ROWS, A_DH))],
            out_specs=per_row((A_DH, LANE)),
            scratch_shapes=[pltpu.VMEM((1, LANE), F32), pltpu.VMEM((1, LANE), F32), pltpu.VMEM((A_DH, LANE), F32)],
        ),
        out_shape=jax.ShapeDtypeStruct((bd, A_DH, LANE), F32),
        compiler_params=_cparams(("parallel", "arbitrary")),
        name="dsa_sample_attn",
    )(page_table, *([pk] * ATT_PAGES), *([pv] * ATT_PAGES), qt, sel, sel_new, k_new, v_new)
    return o_t[:, :, :A_HEADS].transpose(0, 2, 1).reshape(bd, A_HEADS * A_DH).astype(BF16)


def _gate_mix_kernel(rn_ref, att_ref, wr_ref, wa_ref, gr_ref, ga_ref, o_ref):
    r = jnp.dot(rn_ref[...], wr_ref[...], preferred_element_type=F32)
    a = jnp.dot(att_ref[...], wa_ref[...], preferred_element_type=F32)
    o_ref[...] = (gr_ref[...].astype(F32) * r + ga_ref[...].astype(F32) * a).astype(o_ref.dtype)


def _out_proj_kernel(x_ref, mix_ref, wo_ref, nw_ref, h_ref, hn_ref):
    h = x_ref[...] + jnp.dot(mix_ref[...], wo_ref[...], preferred_element_type=F32)
    h_ref[...] = h
    ms = jnp.mean(h * h, axis=-1, keepdims=True)
    hn_ref[...] = (h * lax.rsqrt(ms + EPS) * nw_ref[...]).astype(hn_ref.dtype)


def _mixer_out(x2d, rn, att, gates, wr, wa, wo, norm2_w, tm):
    n, d = x2d.shape
    tn = 512
    ncol = d // tn
    mixed = pl.pallas_call(
        _gate_mix_kernel,
        grid=(ncol, n // tm),
        in_specs=[
            pl.BlockSpec((tm, rn.shape[1]), lambda j, i: (i, 0)),
            pl.BlockSpec((tm, att.shape[1]), lambda j, i: (i, 0)),
            pl.BlockSpec((wr.shape[0], tn), lambda j, i: (0, j)),
            pl.BlockSpec((wa.shape[0], tn), lambda j, i: (0, j)),
            pl.BlockSpec((tm, tn), lambda j, i: (i, j)),
            pl.BlockSpec((tm, tn), lambda j, i: (i, ncol + j)),
        ],
        out_specs=pl.BlockSpec((tm, tn), lambda j, i: (i, j)),
        out_shape=jax.ShapeDtypeStruct((n, d), BF16),
        compiler_params=_cparams(("parallel", "parallel")),
        name="gate_mix",
    )(rn, att, wr, wa, gates, gates)
    tm2 = min(tm, 256)
    h, hn = pl.pallas_call(
        _out_proj_kernel,
        grid=(n // tm2,),
        in_specs=[
            pl.BlockSpec((tm2, d), lambda i: (i, 0)),
            pl.BlockSpec((tm2, d), lambda i: (i, 0)),
            pl.BlockSpec((d, d), lambda i: (0, 0)),
            pl.BlockSpec((1, d), lambda i: (0, 0)),
        ],
        out_specs=[pl.BlockSpec((tm2, d), lambda i: (i, 0)), pl.BlockSpec((tm2, d), lambda i: (i, 0))],
        out_shape=[jax.ShapeDtypeStruct((n, d), F32), jax.ShapeDtypeStruct((n, d), BF16)],
        compiler_params=_cparams(("parallel",)),
        name="out_proj",
    )(x2d, mixed, wo, norm2_w.reshape(1, d))
    return h, hn


PEER_TE = 1024
NO_RANK = 127.0


def _top_desc(x, count):
    vals = []
    cur = x
    for _ in range(count):
        mx = jnp.max(cur, axis=0, keepdims=True)
        vals.append(mx)
        cur = jnp.where(cur == mx, -jnp.inf, cur)
    return vals


def _top_desc_rank(x, count):
    vals = []
    cur = x
    rank = jnp.full(x.shape, NO_RANK, F32)
    for i in range(count):
        mx = jnp.max(cur, axis=0, keepdims=True)
        hit = cur == mx
        vals.append(mx)
        rank = jnp.where(hit, float(i), rank)
        cur = jnp.where(hit, -jnp.inf, cur)
    return vals, rank


def _peer_route_kernel(q_ref, k1_ref, k2_ref, n1_ref, e1_ref, r2_ref, e2_ref):
    nt = (((1,), (1,)), ((), ()))
    half = P_TOPK // 2
    for h in range(P_HEADS):
        qh = q_ref[:, h * P_DKEY:(h + 1) * P_DKEY]
        s1 = lax.dot_general(k1_ref[h], qh, nt, preferred_element_type=F32)
        s2 = lax.dot_general(k2_ref[h], qh, nt, preferred_element_type=F32)
        v1 = _top_desc(s1, P_TOPK)
        v2, r2 = _top_desc_rank(s2, P_TOPK)
        v1blk = jnp.concatenate(v1, axis=0)
        v2blk = jnp.concatenate(v2, axis=0)
        blocks = ([v1[0] + v2blk] + [v1[i] + v2blk[:half] for i in range(1, half)]
                  + [v1blk[half:] + v2[0]])
        c = _top_desc(jnp.concatenate(blocks, axis=0), P_TOPK)
        thr = c[-1]
        z = jnp.zeros_like(thr)
        for i in range(P_TOPK):
            z = z + jnp.exp(c[i] - c[0])
        counts = [jnp.sum(jnp.where(blk >= thr, 1.0, 0.0), axis=0, keepdims=True) for blk in blocks[:half]]
        tail = jnp.where(blocks[half] >= thr, 1.0, 0.0)
        counts += [tail[i:i + 1] for i in range(half)]
        n1 = jnp.zeros_like(s1)
        for i in range(P_TOPK):
            n1 = jnp.where(s1 == v1[i], counts[i], n1)
        n1_ref[h] = n1
        e1_ref[h] = jnp.where(s1 >= v1[-1], jnp.exp(s1 - v1[0]) / z, 0.0)
        r2_ref[h] = r2.astype(r2_ref.dtype)
        e2_ref[h] = jnp.where(r2 < P_TOPK, jnp.exp(s2 - v2[0]), 0.0).astype(e2_ref.dtype)


def _peer_expert_kernel(hn_ref, h_ref, u_ref, vt_ref, n1_ref, e1_ref, r2_ref, e2_ref,
                        y_ref, acc_ref, w_ref, *, a_per_tile):
    j = pl.program_id(1)

    @pl.when(j == 0)
    def _():
        acc_ref[...] = jnp.zeros_like(acc_ref)

    tm = w_ref.shape[1]
    for al in range(a_per_tile):
        a = j * a_per_tile + al
        w = jnp.zeros((P_NKEYS, tm), BF16)
        for h in range(P_HEADS):
            nrow = jnp.broadcast_to(n1_ref[h, pl.ds(a, 1), :].astype(BF16), (P_NKEYS, tm))
            erow = jnp.broadcast_to(e1_ref[h, pl.ds(a, 1), :].astype(BF16), (P_NKEYS, tm))
            w = w + jnp.where(r2_ref[h] < nrow, e2_ref[h], jnp.zeros((), BF16)) * erow
        w_ref[al * P_NKEYS:(al + 1) * P_NKEYS, :] = w

    nt = (((1,), (1,)), ((), ()))
    x = lax.dot_general(u_ref[...], hn_ref[...], nt, preferred_element_type=F32)
    p = w_ref[...] * (0.5 * x * (1.0 + lax.erf(x * (2.0 ** -0.5)))).astype(BF16)
    acc_ref[...] += jnp.dot(vt_ref[...], p, preferred_element_type=F32)

    @pl.when(j == pl.num_programs(1) - 1)
    def _():
        y_ref[...] = h_ref[...] + acc_ref[...].T


def _peer(hn, h, wq_t, k1p, k2p, u_bf, vt_bf, tm, te=PEER_TE):
    n, d = hn.shape
    hk = P_HEADS * P_DKEY
    tr = min(tm, 128)
    tok = lambda i: (0, 0, i)
    head_blk = pl.BlockSpec((P_HEADS, P_NKEYS, tr), tok)
    unused = jnp.zeros((tm, LANE), F32)
    q, = _proj(hn, wq_t, (0, hk), unused, unused, mode="plain", out_dtypes=(BF16,), tm=tm, tn=512)
    n1, e1, r2, e2 = pl.pallas_call(
        _peer_route_kernel,
        grid=(n // tr,),
        in_specs=[
            pl.BlockSpec((tr, hk), lambda i: (i, 0)),
            pl.BlockSpec((P_HEADS, P_NKEYS, P_DKEY), lambda i: (0, 0, 0)),
            pl.BlockSpec((P_HEADS, P_NKEYS, P_DKEY), lambda i: (0, 0, 0)),
        ],
        out_specs=[head_blk] * 4,
        out_shape=[jax.ShapeDtypeStruct((P_HEADS, P_NKEYS, n), dt) for dt in (F32, F32, BF16, BF16)],
        compiler_params=_cparams(("parallel",)),
        name="peer_route",
    )(q, k1p, k2p)

    n_exp = u_bf.shape[0]
    once = pl.Buffered(1)
    tok2 = lambda i, j: (0, 0, i)
    head_f32 = pl.BlockSpec((P_HEADS, P_NKEYS, tm), tok2, pipeline_mode=once)
    head_bf16 = pl.BlockSpec((P_HEADS, P_NKEYS, tm), tok2)
    return pl.pallas_call(
        functools.partial(_peer_expert_kernel, a_per_tile=te // P_NKEYS),
        grid=(n // tm, n_exp // te),
        in_specs=[
            pl.BlockSpec((tm, d), lambda i, j: (i, 0)),
            pl.BlockSpec((tm, d), lambda i, j: (i, 0), pipeline_mode=once),
            pl.BlockSpec((te, d), lambda i, j: (j, 0)),
            pl.BlockSpec((d, te), lambda i, j: (0, j)),
            head_f32, head_f32, head_bf16, head_bf16,
        ],
        out_specs=pl.BlockSpec((tm, d), lambda i, j: (i, 0)),
        out_shape=jax.ShapeDtypeStruct((n, d), F32),
        scratch_shapes=[pltpu.VMEM((d, tm), F32), pltpu.VMEM((te, tm), BF16)],
        compiler_params=_cparams(("parallel", "arbitrary")),
        name="peer_experts",
    )(hn, h, u_bf, vt_bf, n1, e1, r2, e2)


def _prep_weights(w_in, w_ret_o, w_att_o, w_out, peer_wq, peer_keys1, peer_keys2, peer_u, peer_v):
    half = P_DKEY // 2
    k1p = jnp.pad(peer_keys1, ((0, 0), (0, 0), (0, half))).astype(BF16)
    k2p = jnp.pad(peer_keys2, ((0, 0), (0, 0), (half, 0))).astype(BF16)
    return dict(w_t=w_in.T,
                wr=w_ret_o.astype(BF16), wa=w_att_o.astype(BF16), wo=w_out.astype(BF16),
                wq_t=peer_wq.T, k1p=k1p, k2p=k2p,
                u_bf=peer_u.astype(BF16), vt_bf=peer_v.T.astype(BF16))


def _prompt_layer(x, W, norm1_w, q_norm_w, k_norm_w, ret_gn_w, norm2_w):
    bsz, seq, d = x.shape
    x2d = x.reshape(bsz * seq, d)
    tm = 512
    p = _mixer_in(x2d, jnp.arange(seq), norm1_w, W["w_t"], q_norm_w, k_norm_w, 2 * tm)
    s0 = jnp.zeros((bsz, R_HEADS, R_DK, R_DV), F32)
    rn, S = _retention_prompt(p["rq"], p["rk"], p["rvg"], ret_gn_w, s0, bsz, seq)
    att = _dsa_prompt(p["iq"], p["iw"], p["ik_bf"], p["aq"], p["ak_bf"], p["av_bf"], bsz, seq)
    h, hn = _mixer_out(x2d, rn, att, p["gates"], W["wr"], W["wa"], W["wo"], norm2_w, tm)
    y = _peer(hn, h, W["wq_t"], W["k1p"], W["k2p"], W["u_bf"], W["vt_bf"], tm=512)
    return (y.reshape(bsz, seq, d), p["ak"].reshape(bsz, seq, A_KV, A_DH), p["av"].reshape(bsz, seq, A_KV, A_DH),
            p["ik"].reshape(bsz, seq, I_DIM), S)


def _sample_layer(x, pool_k, pool_v, pool_kidx, s_in, page_table, W, norm1_w, q_norm_w, k_norm_w,
                  ret_gn_w, norm2_w):
    bd, ts, d = x.shape
    assert ts == 1
    x2d = x.reshape(bd, d)
    past = page_table.shape[1] * PAGE
    p = _mixer_in(x2d, jnp.full((bd,), past, I32), norm1_w, W["w_t"], q_norm_w, k_norm_w, bd)
    rn, S = _retention_sample(p["rq"], p["rk"], p["rvg"], ret_gn_w, s_in)
    att = _dsa_sample(p["aq"], p["ak"], p["av"], p["iq"], p["iw"], p["ik"], pool_k, pool_v, pool_kidx, page_table)
    h, hn = _mixer_out(x2d, rn, att, p["gates"], W["wr"], W["wa"], W["wo"], norm2_w, bd)
    pad = LANE - bd
    hn_p = jnp.pad(hn, ((0, pad), (0, 0)))
    h_p = jnp.pad(h, ((0, pad), (0, 0)))
    y = _peer(hn_p, h_p, W["wq_t"], W["k1p"], W["k2p"], W["u_bf"], W["vt_bf"], tm=LANE, te=2 * PEER_TE)[:bd]
    return (y.reshape(bd, ts, d), p["ak"].reshape(bd, ts, A_KV, A_DH), p["av"].reshape(bd, ts, A_KV, A_DH),
            p["ik"].reshape(bd, ts, I_DIM), S)


def kernel(x_prompt, x_sample, cache_k, cache_v, cache_k_idx, state_ret, page_table, norm1_w, w_in, q_norm_w,
           k_norm_w, ret_gn_w, w_ret_o, w_att_o, w_out, norm2_w, peer_wq, peer_keys1, peer_keys2, peer_u, peer_v):
    depth = w_in.shape[0]
    assert depth == 1
    l = 0
    W = _prep_weights(w_in[l], w_ret_o[l], w_att_o[l], w_out[l], peer_wq[l], peer_keys1[l], peer_keys2[l],
                      peer_u[l], peer_v[l])
    yp, kp, vp, ikp, sp = _prompt_layer(x_prompt, W, norm1_w[l], q_norm_w[l], k_norm_w[l], ret_gn_w[l], norm2_w[l])
    ys, kn, vn, ikn, sn = _sample_layer(x_sample, cache_k[l], cache_v[l], cache_k_idx[l], state_ret[l], page_table,
                                        W, norm1_w[l], q_norm_w[l], k_norm_w[l], ret_gn_w[l], norm2_w[l])
    return (yp, ys, kp[None], vp[None], ikp[None], sp[None], kn[None], vn[None], ikn[None], sn[None])
```
